```python
import jax, jax.numpy as jnp
from jax import lax
import numpy as np

D_MODEL = 4096
BATCH = 4
SEQ = 2048
DEPTH = 4
DEC_BATCH = 8
DEC_SEQ = 8
PAST_LEN = 8192
PAGE_SIZE = 128

N_META = 16
N_MIXERS = 2
N_A_LAYERS = (DEPTH + 1) // 2
N_B_LAYERS = DEPTH // 2
A_HEADS = 32
A_HEAD_DIM = D_MODEL // A_HEADS
A_WIDTH = A_HEADS * A_HEAD_DIM
A_IN = 4 * A_WIDTH + A_HEADS
Q_BLOCK = 128
B_HEADS = 8
B_KEY_WIDTH = D_MODEL // 2
B_VAL_WIDTH = D_MODEL
B_KEY_DIM = B_KEY_WIDTH // B_HEADS
B_VAL_DIM = B_VAL_WIDTH // B_HEADS
B_GATE_RANK = 16
B_GATE_TAU = 16.0
B_CHUNK = 64
B_IN = 2 * B_KEY_WIDTH + 2 * B_VAL_WIDTH + B_GATE_RANK
EPS = 1e-6

kernel_name = "fox_gla_interleaved_meta_decoder_step"


def rmsnorm(x, g):
    xf = x.astype(jnp.float32)
    y = xf * lax.rsqrt(jnp.mean(xf * xf, axis=-1, keepdims=True) + EPS)
    return (y * g.astype(jnp.float32)).astype(x.dtype)


def page_gather(pool, page_table):
    g = pool[page_table]
    db, n_pages, ps = g.shape[:3]
    return g.reshape((db, n_pages * ps) + g.shape[3:])


def fox_project(u, w_in, b_f):
    z = u @ w_in
    q, k, v, g, f = jnp.split(z, [A_WIDTH, 2 * A_WIDTH, 3 * A_WIDTH, 4 * A_WIDTH], axis=-1)
    shp = u.shape[:2] + (A_HEADS, A_HEAD_DIM)
    logf = jax.nn.log_sigmoid((f + b_f).astype(jnp.float32))
    return q.reshape(shp), k.reshape(shp), v.reshape(shp), g, logf


def fox_prompt(q, k, v, logf):
    B, T = q.shape[:2]
    scale = A_HEAD_DIM ** -0.5
    c = jnp.cumsum(logf, axis=1)
    n_blk = -(-T // Q_BLOCK)
    pad = n_blk * Q_BLOCK - T
    pad4 = ((0, 0), (0, pad), (0, 0), (0, 0))
    q_p, k_p, v_p = jnp.pad(q, pad4), jnp.pad(k, pad4), jnp.pad(v, pad4)
    c_p = jnp.pad(c, ((0, 0), (0, pad), (0, 0)))
    c_kT = c_p.transpose(0, 2, 1)
    pos = jnp.arange(n_blk * Q_BLOCK)
    qb = q_p.reshape(B, n_blk, Q_BLOCK, A_HEADS, A_HEAD_DIM).swapaxes(0, 1)
    cb = c_p.reshape(B, n_blk, Q_BLOCK, A_HEADS).swapaxes(0, 1)
    pb = pos.reshape(n_blk, Q_BLOCK)

    def block(args):
        qi, ci, pi = args
        s = jnp.einsum('bqhd,bkhd->bhqk', qi, k_p).astype(jnp.float32) * scale
        s = s + ci.transpose(0, 2, 1)[:, :, :, None] - c_kT[:, :, None, :]
        mask = pos[None, :] <= pi[:, None]
        s = jnp.where(mask[None, None], s, -jnp.inf)
        p = jax.nn.softmax(s, axis=-1)
        return jnp.einsum('bhqk,bkhd->bqhd', p.astype(v_p.dtype), v_p)

    o = lax.map(block, (qb, cb, pb))
    return o.swapaxes(0, 1).reshape(B, n_blk * Q_BLOCK, A_HEADS, A_HEAD_DIM)[:, :T]


def fox_sample(q, k, v, logf, k_past, v_past, logf_past):
    scale = A_HEAD_DIM ** -0.5
    S = q.shape[1]
    lfp = logf_past.astype(jnp.float32)
    cn = jnp.cumsum(logf, axis=1)
    suff = lax.cumsum(lfp, axis=1, reverse=True) - lfp
    cnT = cn.transpose(0, 2, 1)
    s_past = jnp.einsum('bqhd,bkhd->bhqk', q, k_past).astype(jnp.float32) * scale
    s_past = s_past + cnT[:, :, :, None] + suff.transpose(0, 2, 1)[:, :, None, :]
    s_new = jnp.einsum('bqhd,bkhd->bhqk', q, k).astype(jnp.float32) * scale
    s_new = s_new + cnT[:, :, :, None] - cnT[:, :, None, :]
    causal = jnp.tril(jnp.ones((S, S), dtype=bool))
    s_new = jnp.where(causal[None, None], s_new, -jnp.inf)
    p = jax.nn.softmax(jnp.concatenate([s_past, s_new], axis=-1), axis=-1)
    P = k_past.shape[1]
    o = jnp.einsum('bhqk,bkhd->bqhd', p[..., :P].astype(v.dtype), v_past.astype(v.dtype))
    return o + jnp.einsum('bhqk,bkhd->bqhd', p[..., P:].astype(v.dtype), v)


def fox_out(o, g, w_out):
    B, T = o.shape[:2]
    o = o.reshape(B, T, A_WIDTH).astype(g.dtype)
    return (o * jax.nn.silu(g)) @ w_out


def gla_project(u, w_in, w_a2, b_a):
    z = u @ w_in
    q, k, v, g, a = jnp.split(z, [B_KEY_WIDTH, 2 * B_KEY_WIDTH, 2 * B_KEY_WIDTH + B_VAL_WIDTH,
                                  2 * B_KEY_WIDTH + 2 * B_VAL_WIDTH], axis=-1)
    B, T = u.shape[:2]
    la = jax.nn.log_sigmoid((a @ w_a2 + b_a).astype(jnp.float32)) / B_GATE_TAU
    kshp = (B, T, B_HEADS, B_KEY_DIM)
    q = q.reshape(kshp) * (B_KEY_DIM ** -0.5)
    return q, k.reshape(kshp), v.reshape(B, T, B_HEADS, B_VAL_DIM), g, la.reshape(kshp)


def gla_chunk(S, q, k, v, la):
    S = S.astype(jnp.float32)
    q, k, v = q.astype(jnp.float32), k.astype(jnp.float32), v.astype(jnp.float32)
    C = q.shape[1]
    b = jnp.cumsum(la, axis=1)
    o_inter = jnp.einsum('bchk,bhkv->bchv', q * jnp.exp(b), S)
    causal = jnp.tril(jnp.ones((C, C), dtype=bool))
    diff = b[:, :, None] - b[:, None, :]
    decay = jnp.exp(jnp.where(causal[None, :, :, None, None], diff, -jnp.inf))
    A = jnp.einsum('bthk,bshk,btshk->bhts', q, k, decay)
    o_intra = jnp.einsum('bhts,bshv->bthv', A, v)
    b_last = b[:, -1]
    S_new = jnp.exp(b_last)[..., None] * S + jnp.einsum(
        'bshk,bshv->bhkv', k * jnp.exp(b_last[:, None] - b), v)
    return S_new, o_inter + o_intra


def gla_prompt(q, k, v, la):
    B, T = q.shape[:2]
    S0 = jnp.zeros((B, B_HEADS, B_KEY_DIM, B_VAL_DIM), jnp.float32)
    S1, o_meta = gla_chunk(S0, q[:, :N_META], k[:, :N_META], v[:, :N_META], la[:, :N_META])
    n = (T - N_META) // B_CHUNK

    def to_chunks(a):
        return a[:, N_META:].reshape((B, n, B_CHUNK) + a.shape[2:]).swapaxes(0, 1)

    S_fin, o_rest = lax.scan(lambda s, xs: gla_chunk(s, *xs), S1,
                             (to_chunks(q), to_chunks(k), to_chunks(v), to_chunks(la)))
    o_rest = o_rest.swapaxes(0, 1).reshape(B, T - N_META, B_HEADS, B_VAL_DIM)
    return jnp.concatenate([o_meta, o_rest], axis=1), S_fin


def gla_out(o, g, onorm, w_out):
    B, T = o.shape[:2]
    o = o * lax.rsqrt(jnp.mean(o * o, axis=-1, keepdims=True) + EPS) * onorm.astype(jnp.float32)
    o = o.reshape(B, T, B_VAL_WIDTH) * jax.nn.silu(g.astype(jnp.float32))
    return o.astype(g.dtype) @ w_out


def setup_inputs(seed: int = 0) -> dict:
    key = jax.random.key(seed)
    ks = jax.random.split(key, 24)
    n_pages = PAST_LEN // PAGE_SIZE
    n_used = DEC_BATCH * n_pages
    n_pool = n_used + max(1, n_used // 4)
    f32 = jnp.float32
    nrm = lambda k, s, sc: jax.random.normal(k, s, f32) * sc
    page_table = jax.random.permutation(ks[0], n_pool)[:n_used].reshape(DEC_BATCH, n_pages).astype(jnp.int32)
    return {
        "x_prompt": nrm(ks[1], (BATCH, SEQ, D_MODEL), 1.0),
        "x_sample": nrm(ks[2], (DEC_BATCH, DEC_SEQ, D_MODEL), 1.0),
        "cache_k": nrm(ks[3], (N_A_LAYERS, n_pool, PAGE_SIZE, A_HEADS, A_HEAD_DIM), 1.0),
        "cache_v": nrm(ks[4], (N_A_LAYERS, n_pool, PAGE_SIZE, A_HEADS, A_HEAD_DIM), 1.0),
        "cache_logf": jax.nn.log_sigmoid(3.0 + nrm(ks[5], (N_A_LAYERS, n_pool, PAGE_SIZE, A_HEADS), 1.0)),
        "state_gla": nrm(ks[6], (N_B_LAYERS, DEC_BATCH, B_HEADS, B_KEY_DIM, B_VAL_DIM), 2.0),
        "page_table": page_table,
        "meta_tokens": nrm(ks[7], (N_META, D_MODEL), 1.0),
        "norm_a": 1.0 + nrm(ks[8], (N_A_LAYERS, D_MODEL), 0.02),
        "w_in_a": nrm(ks[9], (N_A_LAYERS, D_MODEL, A_IN), D_MODEL ** -0.5),
        "b_forget": 3.0 + nrm(ks[10], (N_A_LAYERS, A_HEADS), 0.5),
        "w_out_a": nrm(ks[11], (N_A_LAYERS, A_WIDTH, D_MODEL), A_WIDTH ** -0.5),
        "norm_b": 1.0 + nrm(ks[12], (N_B_LAYERS, D_MODEL), 0.02),
        "w_in_b": nrm(ks[13], (N_B_LAYERS, D_MODEL, B_IN), D_MODEL ** -0.5),
        "w_gate_up": nrm(ks[14], (N_B_LAYERS, B_GATE_RANK, B_KEY_WIDTH), B_GATE_RANK ** -0.5),
        "b_gate": nrm(ks[15], (N_B_LAYERS, B_KEY_WIDTH), 0.1),
        "onorm_b": 1.0 + nrm(ks[16], (N_B_LAYERS, B_VAL_DIM), 0.02),
        "w_out_b": nrm(ks[17], (N_B_LAYERS, B_VAL_WIDTH, D_MODEL), B_VAL_WIDTH ** -0.5),
        "final_norm": 1.0 + nrm(ks[18], (D_MODEL,), 0.02),
    }


def reference(x_prompt, x_sample, cache_k, cache_v, cache_logf, state_gla, page_table, meta_tokens,
              norm_a, w_in_a, b_forget, w_out_a, norm_b, w_in_b, w_gate_up, b_gate, onorm_b, w_out_b,
              final_norm):
    B = x_prompt.shape[0]
    meta = jnp.broadcast_to(meta_tokens.astype(x_prompt.dtype)[None], (B, N_META, x_prompt.shape[-1]))
    h_p = jnp.concatenate([meta, x_prompt], axis=1)
    h_s = x_sample
    nk_p, nv_p, nf_p, nk_s, nv_s, nf_s, ns_p, ns_s = [], [], [], [], [], [], [], []
    for i in range(DEPTH):
        j = i // N_MIXERS
        if i % N_MIXERS == 0:
            q, k, v, g, lf = fox_project(rmsnorm(h_p, norm_a[j]), w_in_a[j], b_forget[j])
            h_p = h_p + fox_out(fox_prompt(q, k, v, lf), g, w_out_a[j])
            nk_p.append(k); nv_p.append(v); nf_p.append(lf)
            q, k, v, g, lf = fox_project(rmsnorm(h_s, norm_a[j]), w_in_a[j], b_forget[j])
            o = fox_sample(q, k, v, lf, page_gather(cache_k[j], page_table),
                           page_gather(cache_v[j], page_table), page_gather(cache_logf[j], page_table))
            h_s = h_s + fox_out(o, g, w_out_a[j])
            nk_s.append(k); nv_s.append(v); nf_s.append(lf)
        else:
            q, k, v, g, la = gla_project(rmsnorm(h_p, norm_b[j]), w_in_b[j], w_gate_up[j], b_gate[j])
            o, S_p = gla_prompt(q, k, v, la)
            h_p = h_p + gla_out(o, g, onorm_b[j], w_out_b[j])
            ns_p.append(S_p)
            q, k, v, g, la = gla_project(rmsnorm(h_s, norm_b[j]), w_in_b[j], w_gate_up[j], b_gate[j])
            S_s, o = gla_chunk(state_gla[j], q, k, v, la)
            h_s = h_s + gla_out(o, g, onorm_b[j], w_out_b[j])
            ns_s.append(S_s)
    y_prompt = rmsnorm(h_p, final_norm)[:, N_META:]
    y_sample = rmsnorm(h_s, final_norm)
    return (y_prompt, y_sample, jnp.stack(nk_p), jnp.stack(nv_p), jnp.stack(nf_p),
            jnp.stack(nk_s), jnp.stack(nv_s), jnp.stack(nf_s), jnp.stack(ns_p), jnp.stack(ns_s))
```

```python
import functools

import jax
import jax.numpy as jnp
from jax import lax
from jax.experimental import pallas as pl
from jax.experimental.pallas import tpu as pltpu

EPS = 1e-6
N_META = 16
GLA_GATE_TAU = 16.0
LANES = 128
VMEM_LIMIT_BYTES = 56 * 1024 * 1024
F32 = jnp.float32
BF16 = jnp.bfloat16
HIGHEST = lax.Precision.HIGHEST


def _cparams(*sem):
    return pltpu.CompilerParams(dimension_semantics=sem, vmem_limit_bytes=VMEM_LIMIT_BYTES)


def _pick_tile(n, target, mult):
    best = None
    for t in range(mult, min(n, target) + 1, mult):
        if n % t == 0:
            best = t
    return best if best is not None else n


def _dot(a, b):
    return jnp.dot(a, b, preferred_element_type=F32)


def _dot_nt(a, b, precision=None):
    return lax.dot_general(a, b, (((1,), (1,)), ((), ())), precision=precision,
                           preferred_element_type=F32)


def _dot_tn(a, b):
    return lax.dot_general(a, b, (((0,), (0,)), ((), ())), preferred_element_type=F32)


def _dot_exact(a, b):
    return jnp.dot(a, b, precision=HIGHEST, preferred_element_type=F32)


def _log_sigmoid(x):
    return jnp.minimum(x, 0.0) - jnp.log1p(jnp.exp(-jnp.abs(x)))


def _silu(x):
    return x * (1.0 / (1.0 + jnp.exp(-x)))


def _iota2(shape, dim):
    return lax.broadcasted_iota(jnp.int32, shape, dim)


def _rmsnorm_kernel(x_ref, g_ref, o_ref):
    x = x_ref[...]
    ms = jnp.mean(x * x, axis=-1, keepdims=True)
    o_ref[...] = (x * lax.rsqrt(ms + EPS) * g_ref[...]).astype(o_ref.dtype)


def _rmsnorm(x, g, out_dtype):
    m, d = x.shape
    tm = _pick_tile(m, 512, 16)
    return pl.pallas_call(
        _rmsnorm_kernel,
        grid=(m // tm,),
        in_specs=[pl.BlockSpec((tm, d), lambda i: (i, 0)),
                  pl.BlockSpec((1, d), lambda i: (0, 0))],
        out_specs=pl.BlockSpec((tm, d), lambda i: (i, 0)),
        out_shape=jax.ShapeDtypeStruct((m, d), out_dtype),
        compiler_params=_cparams("arbitrary"),
        name="rmsnorm",
    )(x, g.reshape(1, d))


def _proj_kernel(x_ref, w_ref, o_ref, wb_ref):
    @pl.when(pl.program_id(1) == 0)
    def _():
        wb_ref[...] = w_ref[...].astype(BF16)

    o_ref[...] = _dot(x_ref[...], wb_ref[...]).astype(o_ref.dtype)


def _proj_res_kernel(x_ref, w_ref, r_ref, o_ref, wb_ref):
    @pl.when(pl.program_id(1) == 0)
    def _():
        wb_ref[...] = w_ref[...].astype(BF16)

    o_ref[...] = r_ref[...] + _dot(x_ref[...], wb_ref[...])


def _proj(x, w, layer, col0, ncols, out_dtype, residual=None, tn=512, tm_target=704):
    m, k = x.shape
    tm = _pick_tile(m, tm_target, 16)
    assert ncols % tn == 0 and col0 % tn == 0
    cb = col0 // tn
    in_specs = [pl.BlockSpec((tm, k), lambda n, i: (i, 0)),
                pl.BlockSpec((None, k, tn), lambda n, i: (layer, 0, cb + n))]
    args = [x, w]
    kern = _proj_kernel
    aliases = {}
    if residual is not None:
        in_specs.append(pl.BlockSpec((tm, tn), lambda n, i: (i, n)))
        args.append(residual)
        kern = _proj_res_kernel
        aliases = {2: 0}
    return pl.pallas_call(
        kern,
        grid=(ncols // tn, m // tm),
        in_specs=in_specs,
        out_specs=pl.BlockSpec((tm, tn), lambda n, i: (i, n)),
        out_shape=jax.ShapeDtypeStruct((m, ncols), out_dtype),
        scratch_shapes=[pltpu.VMEM((k, tn), BF16)],
        input_output_aliases=aliases,
        compiler_params=_cparams("arbitrary", "arbitrary"),
        name="proj",
    )(*args)


def _tail_kernel(x_ref, w_ref, o_ref):
    o_ref[...] = _dot(x_ref[...], w_ref[...])


def _tail_proj(x, w_tail):
    m, k = x.shape
    tm = _pick_tile(m, 704, 16)
    return pl.pallas_call(
        _tail_kernel,
        grid=(m // tm,),
        in_specs=[pl.BlockSpec((tm, k), lambda i: (i, 0)),
                  pl.BlockSpec((k, LANES), lambda i: (0, 0))],
        out_specs=pl.BlockSpec((tm, LANES), lambda i: (i, 0)),
        out_shape=jax.ShapeDtypeStruct((m, LANES), F32),
        compiler_params=_cparams("arbitrary"),
        name="tail_proj",
    )(x, w_tail)


def _logf_kernel(z_ref, b_ref, lf_ref, c_ref, carry_ref, *, seg):
    tr = z_ref.shape[0]
    lf = _log_sigmoid(z_ref[...] + b_ref[...])
    lf_ref[...] = lf
    row = _iota2((tr, tr), 0)
    col = _iota2((tr, tr), 1)
    if seg >= tr:
        @pl.when(pl.program_id(1) == 0)
        def _():
            carry_ref[...] = jnp.zeros_like(carry_ref)

        tri = (col <= row).astype(F32)
        c = _dot_exact(tri, lf) + carry_ref[...]
        c_ref[...] = c
        carry_ref[...] = c[tr - 1:tr, :]
    else:
        tri = ((col <= row) & ((row // seg) == (col // seg))).astype(F32)
        c_ref[...] = _dot_exact(tri, lf)


def _logf_cumsum(z, bias, nb, t):
    m = nb * t
    if t >= 128:
        tr = _pick_tile(t, 704, 8)
        grid = (nb, t // tr)
        spec = pl.BlockSpec((tr, LANES), lambda b, i: (b * (t // tr) + i, 0))
        seg = t
    else:
        tr = m
        grid = (1, 1)
        spec = pl.BlockSpec((tr, LANES), lambda b, i: (0, 0))
        seg = t
    return pl.pallas_call(
        functools.partial(_logf_kernel, seg=seg),
        grid=grid,
        in_specs=[spec, pl.BlockSpec((1, LANES), lambda b, i: (0, 0))],
        out_specs=[spec, spec],
        out_shape=[jax.ShapeDtypeStruct((m, LANES), F32)] * 2,
        scratch_shapes=[pltpu.VMEM((1, LANES), F32)],
        compiler_params=_cparams("arbitrary", "arbitrary"),
        name="logf_cumsum",
    )(z, bias)


def _fox_prompt_kernel(q_ref, k_ref, v_ref, g_ref, cm_ref, cs_ref, y_ref, kb_ref, vb_ref,
                       *, n_meta, tq, scale):
    t_len = q_ref.shape[0]
    nblk = (t_len - n_meta) // tq
    kb_ref[...] = k_ref[...].astype(BF16)
    vb_ref[...] = v_ref[...].astype(BF16)
    km = kb_ref[0:n_meta, :]
    vm = vb_ref[0:n_meta, :]
    cm = cm_ref[...]

    def emit(r0, n, acc, l):
        o = acc * (1.0 / l)
        y_ref[pl.ds(r0, n), :] = (o * _silu(g_ref[pl.ds(r0, n), :])).astype(y_ref.dtype)

    s = _dot_nt(q_ref[0:n_meta, :], km) * scale - cm
    s = jnp.where(_iota2(s.shape, 1) <= _iota2(s.shape, 0), s, -jnp.inf)
    m = jnp.max(s, axis=1, keepdims=True)
    p = jnp.exp(s - m)
    emit(0, n_meta, _dot(p.astype(BF16), vm), jnp.sum(p, axis=1, keepdims=True))

    def qblock(i, carry):
        r0 = pl.multiple_of(n_meta + i * tq, 16)
        q = q_ref[pl.ds(r0, tq), :]
        s0 = _dot_nt(q, km) * scale - cm
        m0 = jnp.max(s0, axis=1, keepdims=True)
        p0 = jnp.exp(s0 - m0)
        l0 = jnp.sum(p0, axis=1, keepdims=True)
        acc0 = _dot(p0.astype(BF16), vm)

        def step(j, mla, masked):
            m_i, l_i, acc = mla
            k0 = pl.multiple_of(n_meta + j * tq, 16)
            c0 = pl.multiple_of(j * tq, tq)
            s_ = _dot_nt(q, kb_ref[pl.ds(k0, tq), :]) * scale - cs_ref[:, pl.ds(c0, tq)]
            if masked:
                s_ = jnp.where(_iota2(s_.shape, 1) <= _iota2(s_.shape, 0), s_, -jnp.inf)
            m_n = jnp.maximum(m_i, jnp.max(s_, axis=1, keepdims=True))
            alpha = jnp.exp(m_i - m_n)
            p_ = jnp.exp(s_ - m_n)
            l_n = alpha * l_i + jnp.sum(p_, axis=1, keepdims=True)
            acc_n = alpha * acc + _dot(p_.astype(BF16), vb_ref[pl.ds(k0, tq), :])
            return m_n, l_n, acc_n

        mla = lax.fori_loop(0, i, lambda j, c: step(j, c, False), (m0, l0, acc0))
        _, l_f, acc_f = step(i, mla, True)
        emit(r0, tq, acc_f, l_f)
        return carry

    lax.fori_loop(0, nblk, qblock, 0)


def _fox_prompt(q, k, v, g, c_meta, c_seq, n_heads):
    b, t, w = q.shape
    dh = w // n_heads
    s_len = t - N_META
    tq = _pick_tile(s_len, 256, 128)
    blk = lambda: pl.BlockSpec((None, t, dh), lambda bi, hi: (bi, 0, hi))
    return pl.pallas_call(
        functools.partial(_fox_prompt_kernel, n_meta=N_META, tq=tq, scale=dh ** -0.5),
        grid=(b, n_heads),
        in_specs=[blk(), blk(), blk(), blk(),
                  pl.BlockSpec((None, None, 1, N_META), lambda bi, hi: (bi, hi, 0, 0)),
                  pl.BlockSpec((None, None, 1, s_len), lambda bi, hi: (bi, hi, 0, 0))],
        out_specs=blk(),
        out_shape=jax.ShapeDtypeStruct((b, t, w), BF16),
        scratch_shapes=[pltpu.VMEM((t, dh), BF16), pltpu.VMEM((t, dh), BF16)],
        compiler_params=_cparams("arbitrary", "arbitrary"),
        name="fox_prompt_attn",
    )(q, k, v, g, c_meta, c_seq)


def _fox_sample_kernel(pt_ref, q_ref, kn_ref, vn_ref, gn_ref, cn_ref, kp_ref, vp_ref, lfp_ref,
                       y_ref, wt_ref, m_ref, l_ref, carry_ref, acc_ref,
                       *, n_heads, n_new, scale):
    del pt_ref
    p_idx = pl.program_id(1)
    n_pages = pl.num_programs(1)
    rows = wt_ref.shape[0]
    width = wt_ref.shape[1]
    dh = width // n_heads
    page = kp_ref.shape[0]
    head_of_row = _iota2((rows, n_heads), 0) % n_heads
    expand = (head_of_row == _iota2((rows, n_heads), 1)).astype(F32)

    @pl.when(p_idx == 0)
    def _():
        qv = q_ref[...]
        hmask = (_iota2((n_heads, width), 1) // dh) == _iota2((n_heads, width), 0)
        for qi in range(n_new):
            blk = jnp.where(hmask, jnp.broadcast_to(qv[qi:qi + 1, :], (n_heads, width)), 0.0)
            wt_ref[qi * n_heads:(qi + 1) * n_heads, :] = blk.astype(BF16)
        m_ref[...] = jnp.full_like(m_ref, -jnp.inf)
        l_ref[...] = jnp.zeros_like(l_ref)
        carry_ref[...] = jnp.zeros_like(carry_ref)
        acc_ref[...] = jnp.zeros_like(acc_ref)

    def update(s, vb):
        m_i = m_ref[...]
        m_n = jnp.maximum(m_i, jnp.max(s, axis=1, keepdims=True))
        alpha = jnp.exp(m_i - m_n)
        p = jnp.exp(s - m_n)
        l_ref[...] = alpha * l_ref[...] + jnp.sum(p, axis=1, keepdims=True)
        acc_ref[...] = alpha * acc_ref[...] + _dot(p.astype(BF16), vb)
        m_ref[...] = m_n

    lfe = _dot_nt(expand, lfp_ref[...], precision=HIGHEST)
    later = (_iota2((page, page), 0) > _iota2((page, page), 1)).astype(F32)
    suff = _dot_exact(lfe, later) + carry_ref[...]
    carry_ref[...] = carry_ref[...] + jnp.sum(lfe, axis=1, keepdims=True)
    s_past = _dot_nt(wt_ref[...], kp_ref[...].astype(BF16)) * scale + suff
    update(s_past, vp_ref[...].astype(BF16))

    @pl.when(p_idx == n_pages - 1)
    def _():
        n_pad = kn_ref.shape[0]
        bias = _dot_nt(expand, cn_ref[...], precision=HIGHEST)
        s_new = _dot_nt(wt_ref[...], kn_ref[...].astype(BF16)) * scale - bias
        q_of_row = _iota2((rows, n_pad), 0) // n_heads
        s_new = jnp.where(_iota2((rows, n_pad), 1) <= q_of_row, s_new, -jnp.inf)
        update(s_new, vn_ref[...].astype(BF16))
        o_all = acc_ref[...] * (1.0 / l_ref[...])
        hmask = (_iota2((n_heads, width), 1) // dh) == _iota2((n_heads, width), 0)
        gate = _silu(gn_ref[...])
        for qi in range(n_new):
            blk = jnp.where(hmask, o_all[qi * n_heads:(qi + 1) * n_heads, :], 0.0)
            o_q = jnp.sum(blk, axis=0, keepdims=True)
            y_ref[qi:qi + 1, :] = (o_q * gate[qi:qi + 1, :]).astype(y_ref.dtype)


def _fox_sample(q, k_new, v_new, g_new, c_new, cache_k, cache_v, cache_lf, layer, page_table, n_heads):
    db, s_new, w = q.shape
    n_pages = page_table.shape[1]
    ps = cache_k.shape[2]
    dh = w // n_heads
    n_pad = 16
    pad = ((0, 0), (0, n_pad - s_new), (0, 0))
    k_pad, v_pad, c_pad = jnp.pad(k_new, pad), jnp.pad(v_new, pad), jnp.pad(c_new, pad)
    rows = s_new * n_heads
    per_b = lambda n, last: pl.BlockSpec((None, n, last), lambda b, p, pt: (b, 0, 0))
    page_spec = lambda last: pl.BlockSpec(
        (None, None, ps, last), lambda b, p, pt: (layer, pt[b, n_pages - 1 - p], 0, 0))
    grid_spec = pltpu.PrefetchScalarGridSpec(
        num_scalar_prefetch=1,
        grid=(db, n_pages),
        in_specs=[per_b(s_new, w), per_b(n_pad, w), per_b(n_pad, w), per_b(s_new, w),
                  per_b(n_pad, n_heads), page_spec(w), page_spec(w), page_spec(n_heads)],
        out_specs=per_b(s_new, w),
        scratch_shapes=[pltpu.VMEM((rows, w), BF16), pltpu.VMEM((rows, 1), F32),
                        pltpu.VMEM((rows, 1), F32), pltpu.VMEM((rows, 1), F32),
                        pltpu.VMEM((rows, w), F32)],
    )
    return pl.pallas_call(
        functools.partial(_fox_sample_kernel, n_heads=n_heads, n_new=s_new, scale=dh ** -0.5),
        grid_spec=grid_spec,
        out_shape=jax.ShapeDtypeStruct((db, s_new, w), BF16),
        compiler_params=_cparams("arbitrary", "arbitrary"),
        name="fox_sample_attn",
    )(page_table, q, k_pad, v_pad, g_new, c_pad, cache_k, cache_v, cache_lf)


GLA_CHUNK = 128
GLA_DIAG = 8


def _gla_chunk(r0, c, refs, wa, q_scale):
    q_ref, k_ref, v_ref, g_ref, a_ref, bg_ref, on_ref, y_ref, s_ref = refs
    rows = pl.ds(r0, c)
    q = q_ref[rows, :] * q_scale
    k = k_ref[rows, :]
    v = v_ref[rows, :].astype(BF16)
    dk = q.shape[1]
    x = _dot(a_ref[rows, :].astype(BF16), wa) + bg_ref[...]
    la = _log_sigmoid(x) * (1.0 / GLA_GATE_TAU)
    row = _iota2((c, c), 0)
    col = _iota2((c, c), 1)
    b = _dot_exact((col <= row).astype(F32), la)
    b_last = b[c - 1:c, :]

    s_old = s_ref[...]
    o = _dot((q * jnp.exp(b)).astype(BF16), s_old.astype(BF16))

    a_mat = jnp.zeros((c, c), F32)
    diff = row - col
    for d in range(GLA_DIAG):
        if d == 0:
            w = q * k
        else:
            kd = pltpu.roll(k, d, 0)
            bd = pltpu.roll(b, d, 0)
            w = q * kd * jnp.exp(jnp.minimum(b - bd, 0.0))
        a_mat = jnp.where(diff == d, jnp.sum(w, axis=1, keepdims=True), a_mat)
    xor = row ^ col
    blk = 2 * GLA_DIAG
    while blk <= c:
        half = blk // 2
        pieces = [jnp.broadcast_to(b[s0 + half - 1:s0 + half, :], (blk, dk)) for s0 in range(0, c, blk)]
        b_mid = pieces[0] if len(pieces) == 1 else jnp.concatenate(pieces, axis=0)
        q_t = (q * jnp.exp(jnp.minimum(b - b_mid, 0.0))).astype(BF16)
        k_t = (k * jnp.exp(jnp.minimum(b_mid - b, 0.0))).astype(BF16)
        a_mat = jnp.where(xor >= half, _dot_nt(q_t, k_t), a_mat)
        blk *= 2
    a_mat = jnp.where(col <= row, a_mat, 0.0)
    o = o + _dot(a_mat.astype(BF16), v)

    k_end = (k * jnp.exp(b_last - b)).astype(BF16)
    decay = jnp.transpose(jnp.broadcast_to(jnp.exp(b_last), (8, dk)))[:, 0:1]
    s_ref[...] = decay * s_old + _dot_tn(k_end, v)

    o = o * lax.rsqrt(jnp.mean(o * o, axis=-1, keepdims=True) + EPS) * on_ref[...]
    y_ref[rows, :] = (o * _silu(g_ref[rows, :])).astype(y_ref.dtype)


def _gla_kernel(*refs, head, n_chunks, has_state, q_scale):
    if has_state:
        q_ref, k_ref, v_ref, g_ref, a_ref, wa_ref, bg_ref, on_ref, s0_ref, y_ref, so_ref, s_ref = refs
        s_ref[...] = s0_ref[...]
    else:
        q_ref, k_ref, v_ref, g_ref, a_ref, wa_ref, bg_ref, on_ref, y_ref, so_ref, s_ref = refs
        s_ref[...] = jnp.zeros_like(s_ref)
    wa = wa_ref[...].astype(BF16)
    crefs = (q_ref, k_ref, v_ref, g_ref, a_ref, bg_ref, on_ref, y_ref, s_ref)
    if head:
        _gla_chunk(0, head, crefs, wa, q_scale)

    def body(i, carry):
        _gla_chunk(pl.multiple_of(head + i * GLA_CHUNK, 16), GLA_CHUNK, crefs, wa, q_scale)
        return carry

    if n_chunks:
        lax.fori_loop(0, n_chunks, body, 0)
    so_ref[...] = s_ref[...]


def _gla(zq, zk, zv, zg, za, wa2, b_gate, onorm, state0, n_heads):
    b, t, kw = zq.shape
    vw = zv.shape[2]
    dk, dv = kw // n_heads, vw // n_heads
    head = t % GLA_CHUNK
    n_chunks = t // GLA_CHUNK
    assert head in (0, 8, 16, 32, 64)
    tok = lambda last: pl.BlockSpec((None, t, last), lambda bi, hi: (bi, 0, hi))
    in_specs = [tok(dk), tok(dk), tok(dv), tok(dv),
                pl.BlockSpec((None, t, LANES), lambda bi, hi: (bi, 0, 0)),
                pl.BlockSpec((LANES, dk), lambda bi, hi: (0, hi)),
                pl.BlockSpec((1, dk), lambda bi, hi: (0, hi)),
                pl.BlockSpec((1, dv), lambda bi, hi: (0, 0))]
    args = [zq, zk, zv, zg, za, wa2, b_gate, onorm]
    st_spec = pl.BlockSpec((None, None, dk, dv), lambda bi, hi: (bi, hi, 0, 0))
    if state0 is not None:
        in_specs.append(st_spec)
        args.append(state0)
    return pl.pallas_call(
        functools.partial(_gla_kernel, head=head, n_chunks=n_chunks,
                          has_state=state0 is not None, q_scale=dk ** -0.5),
        grid=(b, n_heads),
        in_specs=in_specs,
        out_specs=[tok(dv), st_spec],
        out_shape=[jax.ShapeDtypeStruct((b, t, vw), BF16),
                   jax.ShapeDtypeStruct((b, n_heads, dk, dv), F32)],
        scratch_shapes=[pltpu.VMEM((dk, dv), F32)],
        compiler_params=_cparams("arbitrary", "arbitrary"),
        name="gla",
    )(*args)


def _pad_lanes(x):
    return jnp.pad(x, ((0, 0), (0, LANES - x.shape[1])))


def _fox_layer(h, nb, t, norm, w_in, b_f, w_out, layer, n_heads, sample_ctx):
    d = h.shape[1]
    aw = w_out.shape[1]
    u = _rmsnorm(h, norm[layer], BF16)
    q = _proj(u, w_in, layer, 0, aw, BF16 if sample_ctx is None else F32)
    k = _proj(u, w_in, layer, aw, aw, F32)
    v = _proj(u, w_in, layer, 2 * aw, aw, F32)
    g = _proj(u, w_in, layer, 3 * aw, aw, F32)
    w_tail = _pad_lanes(w_in[layer][:, 4 * aw:]).astype(BF16)
    z_f = _tail_proj(u, w_tail)
    bias = _pad_lanes(b_f[layer][None, :].astype(F32))
    lf, c = _logf_cumsum(z_f, bias, nb, t)
    lf = lf[:, :n_heads].reshape(nb, t, n_heads)
    c = c[:, :n_heads].reshape(nb, t, n_heads)
    shp = (nb, t, aw)
    if sample_ctx is None:
        c_t = jnp.transpose(c, (0, 2, 1))[:, :, None, :]
        y = _fox_prompt(q.reshape(shp), k.reshape(shp), v.reshape(shp), g.reshape(shp),
                        c_t[..., :N_META], c_t[..., N_META:], n_heads)
    else:
        cache_k, cache_v, cache_lf, page_table = sample_ctx
        y = _fox_sample(q.reshape(shp), k.reshape(shp), v.reshape(shp), g.reshape(shp), c,
                        cache_k, cache_v, cache_lf, layer, page_table, n_heads)
    h = _proj(y.reshape(nb * t, aw), w_out, layer, 0, d, F32, residual=h)
    dh = aw // n_heads
    return h, k.reshape(nb, t, n_heads, dh), v.reshape(nb, t, n_heads, dh), lf


def _gla_layer(h, nb, t, norm, w_in, w_a2, b_gate, onorm, w_out, layer, n_heads, state0):
    d = h.shape[1]
    vw = w_out.shape[1]
    kw = w_a2.shape[2]
    u = _rmsnorm(h, norm[layer], BF16)
    zq = _proj(u, w_in, layer, 0, kw, F32)
    zk = _proj(u, w_in, layer, kw, kw, F32)
    zv = _proj(u, w_in, layer, 2 * kw, vw, F32)
    zg = _proj(u, w_in, layer, 2 * kw + vw, vw, F32)
    w_tail = _pad_lanes(w_in[layer][:, 2 * kw + 2 * vw:]).astype(BF16)
    za = _tail_proj(u, w_tail)
    rank = w_a2.shape[1]
    wa2 = jnp.pad(w_a2[layer], ((0, LANES - rank), (0, 0)))
    y, s_new = _gla(zq.reshape(nb, t, kw), zk.reshape(nb, t, kw), zv.reshape(nb, t, vw),
                    zg.reshape(nb, t, vw), za.reshape(nb, t, LANES), wa2,
                    b_gate[layer][None, :], onorm[layer][None, :], state0, n_heads)
    h = _proj(y.reshape(nb * t, vw), w_out, layer, 0, d, F32, residual=h)
    return h, s_new


def kernel(x_prompt, x_sample, cache_k, cache_v, cache_logf, state_gla, page_table, meta_tokens,
           norm_a, w_in_a, b_forget, w_out_a, norm_b, w_in_b, w_gate_up, b_gate, onorm_b, w_out_b,
           final_norm):
    bsz, seq, d = x_prompt.shape
    db, ds, _ = x_sample.shape
    a_heads = b_forget.shape[1]
    b_heads = state_gla.shape[2]
    depth = norm_a.shape[0] + norm_b.shape[0]
    t = N_META + seq
    meta = jnp.broadcast_to(meta_tokens.astype(x_prompt.dtype)[None], (bsz, N_META, d))
    h_p = jnp.concatenate([meta, x_prompt], axis=1).reshape(bsz * t, d)
    h_s = x_sample.reshape(db * ds, d)
    n_pool, ps = cache_k.shape[1], cache_k.shape[2]
    ck = cache_k.reshape(cache_k.shape[0], n_pool, ps, -1)
    cv = cache_v.reshape(cache_v.shape[0], n_pool, ps, -1)
    nk_p, nv_p, nf_p, nk_s, nv_s, nf_s, ns_p, ns_s = [], [], [], [], [], [], [], []
    for i in range(depth):
        j = i // 2
        if i % 2 == 0:
            h_p, k, v, lf = _fox_layer(h_p, bsz, t, norm_a, w_in_a, b_forget, w_out_a, j, a_heads, None)
            nk_p.append(k); nv_p.append(v); nf_p.append(lf)
            h_s, k, v, lf = _fox_layer(h_s, db, ds, norm_a, w_in_a, b_forget, w_out_a, j, a_heads,
                                       (ck, cv, cache_logf, page_table))
            nk_s.append(k); nv_s.append(v); nf_s.append(lf)
        else:
            h_p, s_p = _gla_layer(h_p, bsz, t, norm_b, w_in_b, w_gate_up, b_gate, onorm_b, w_out_b,
                                  j, b_heads, None)
            ns_p.append(s_p)
            h_s, s_s = _gla_layer(h_s, db, ds, norm_b, w_in_b, w_gate_up, b_gate, onorm_b, w_out_b,
                                  j, b_heads, state_gla[j])
            ns_s.append(s_s)
    y_prompt = _rmsnorm(h_p, final_norm, F32).reshape(bsz, t, d)[:, N_META:]
    y_sample = _rmsnorm(h_s, final_norm, F32).reshape(db, ds, d)
    return (y_prompt, y_sample, jnp.stack(nk_p), jnp.stack(nv_p), jnp.stack(nf_p),
            jnp.stack(nk_s), jnp.stack(nv_s), jnp.stack(nf_s), jnp.stack(ns_p), jnp.stack(ns_s))
```

```python
import functools

import jax
import jax.numpy as jnp
from jax import lax
from jax.experimental import pallas as pl
from jax.experimental.pallas import tpu as pltpu

EPS = 1e-6
N_META = 16
GLA_GATE_TAU = 16.0
LANES = 128
SUBLANES = 8
VMEM_LIMIT_BYTES = 56 * 1024 * 1024
F32 = jnp.float32
BF16 = jnp.bfloat16
HIGHEST = lax.Precision.HIGHEST


def _cparams(*sem):
    return pltpu.CompilerParams(dimension_semantics=sem, vmem_limit_bytes=VMEM_LIMIT_BYTES)


def _pick_tile(n, target, mult):
    best = None
    for t in range(mult, min(n, target) + 1, mult):
        if n % t == 0:
            best = t
    return best if best is not None else n


def _dot(a, b):
    return jnp.dot(a, b, preferred_element_type=F32)


def _dot_nt(a, b, precision=None):
    return lax.dot_general(a, b, (((1,), (1,)), ((), ())), precision=precision,
                           preferred_element_type=F32)


def _dot_tn(a, b):
    return lax.dot_general(a, b, (((0,), (0,)), ((), ())), preferred_element_type=F32)


def _dot_exact(a, b):
    return jnp.dot(a, b, precision=HIGHEST, preferred_element_type=F32)


def _log_sigmoid(x):
    return jnp.minimum(x, 0.0) - jnp.log1p(jnp.exp(-jnp.abs(x)))


def _silu(x):
    return x * (1.0 / (1.0 + jnp.exp(-x)))


def _iota2(shape, dim):
    return lax.broadcasted_iota(jnp.int32, shape, dim)


def _rmsnorm_kernel(x_ref, g_ref, o_ref):
    x = x_ref[...]
    ms = jnp.mean(x * x, axis=-1, keepdims=True)
    o_ref[...] = (x * lax.rsqrt(ms + EPS) * g_ref[...]).astype(o_ref.dtype)


def _rmsnorm(x, g, out_dtype):
    m, d = x.shape
    tm = _pick_tile(m, 512, 16)
    return pl.pallas_call(
        _rmsnorm_kernel,
        grid=(m // tm,),
        in_specs=[pl.BlockSpec((tm, d), lambda i: (i, 0)),
                  pl.BlockSpec((1, d), lambda i: (0, 0))],
        out_specs=pl.BlockSpec((tm, d), lambda i: (i, 0)),
        out_shape=jax.ShapeDtypeStruct((m, d), out_dtype),
        compiler_params=_cparams("arbitrary"),
        name="rmsnorm",
    )(x, g.reshape(1, d))


def _proj_kernel(*refs, w_is_nk, has_res):
    x_ref, w_ref = refs[0], refs[1]
    r_ref = refs[2] if has_res else None
    o_ref, wb_ref = refs[-2], refs[-1]

    @pl.when(pl.program_id(1) == 0)
    def _():
        wb_ref[...] = w_ref[...].astype(BF16)

    acc = _dot_nt(x_ref[...], wb_ref[...]) if w_is_nk else _dot(x_ref[...], wb_ref[...])
    if has_res:
        acc = r_ref[...] + acc
    o_ref[...] = acc.astype(o_ref.dtype)


def _proj(x, w, layer, col0, ncols, out_dtype, *, w_is_nk, residual=None, stack=None,
          tn=512, tm_target=704):
    m, k = x.shape
    tm = _pick_tile(m, tm_target, 16)
    assert ncols % tn == 0 and col0 % tn == 0
    cb = col0 // tn
    if w_is_nk:
        w_spec = pl.BlockSpec((None, tn, k), lambda n, i: (layer, cb + n, 0))
        wb_shape = (tn, k)
    else:
        w_spec = pl.BlockSpec((None, k, tn), lambda n, i: (layer, 0, cb + n))
        wb_shape = (k, tn)
    in_specs = [pl.BlockSpec((tm, k), lambda n, i: (i, 0)), w_spec]
    args = [x, w]
    aliases = {}
    if residual is not None:
        in_specs.append(pl.BlockSpec((tm, tn), lambda n, i: (i, n)))
        args.append(residual)
        aliases = {2: 0}
    if stack is None:
        out_spec = pl.BlockSpec((tm, tn), lambda n, i: (i, n))
        out_shape = jax.ShapeDtypeStruct((m, ncols), out_dtype)
    else:
        n_layers, prev = stack
        out_spec = pl.BlockSpec((None, tm, tn), lambda n, i: (layer, i, n))
        out_shape = jax.ShapeDtypeStruct((n_layers, m, ncols), out_dtype)
        if prev is not None:
            in_specs.append(pl.BlockSpec(memory_space=pl.ANY))
            aliases = {len(args): 0}
            args.append(prev)
    kern = functools.partial(_proj_kernel, w_is_nk=w_is_nk, has_res=residual is not None)
    if stack is not None and stack[1] is not None:
        kern = functools.partial(_proj_prev_kernel, inner=kern)
    return pl.pallas_call(
        kern,
        grid=(ncols // tn, m // tm),
        in_specs=in_specs,
        out_specs=out_spec,
        out_shape=out_shape,
        scratch_shapes=[pltpu.VMEM(wb_shape, BF16)],
        input_output_aliases=aliases,
        compiler_params=_cparams("arbitrary", "arbitrary"),
        name="proj",
    )(*args)


def _proj_prev_kernel(x_ref, w_ref, prev_ref, o_ref, wb_ref, *, inner):
    del prev_ref
    inner(x_ref, w_ref, o_ref, wb_ref)


def _tail_kernel(x_ref, w_ref, o_ref):
    o_ref[...] = _dot_nt(x_ref[...], w_ref[...].astype(BF16))


def _tail_proj(x, w_tail):
    m, k = x.shape
    tm = _pick_tile(m, 704, 16)
    return pl.pallas_call(
        _tail_kernel,
        grid=(m // tm,),
        in_specs=[pl.BlockSpec((tm, k), lambda i: (i, 0)),
                  pl.BlockSpec((LANES, k), lambda i: (0, 0))],
        out_specs=pl.BlockSpec((tm, LANES), lambda i: (i, 0)),
        out_shape=jax.ShapeDtypeStruct((m, LANES), F32),
        compiler_params=_cparams("arbitrary"),
        name="tail_proj",
    )(x, w_tail)


def _logf_kernel(z_ref, b_ref, lf_ref, c_ref, carry_ref, *, seg):
    tr = z_ref.shape[0]
    lf = _log_sigmoid(z_ref[...] + b_ref[...])
    lf_ref[...] = lf
    row = _iota2((tr, tr), 0)
    col = _iota2((tr, tr), 1)
    if seg >= tr:
        @pl.when(pl.program_id(1) == 0)
        def _():
            carry_ref[...] = jnp.zeros_like(carry_ref)

        tri = (col <= row).astype(F32)
        c = _dot_exact(tri, lf) + carry_ref[...]
        c_ref[...] = c
        carry_ref[...] = c[tr - 1:tr, :]
    else:
        tri = ((col <= row) & ((row // seg) == (col // seg))).astype(F32)
        c_ref[...] = _dot_exact(tri, lf)


def _logf_cumsum(z, bias, nb, t):
    m = nb * t
    if t >= 128:
        tr = _pick_tile(t, 704, 8)
        grid = (nb, t // tr)
        spec = pl.BlockSpec((tr, LANES), lambda b, i: (b * (t // tr) + i, 0))
    else:
        tr = m
        grid = (1, 1)
        spec = pl.BlockSpec((tr, LANES), lambda b, i: (0, 0))
    return pl.pallas_call(
        functools.partial(_logf_kernel, seg=t),
        grid=grid,
        in_specs=[spec, pl.BlockSpec((1, LANES), lambda b, i: (0, 0))],
        out_specs=[spec, spec],
        out_shape=[jax.ShapeDtypeStruct((m, LANES), F32)] * 2,
        scratch_shapes=[pltpu.VMEM((1, LANES), F32)],
        compiler_params=_cparams("arbitrary", "arbitrary"),
        name="logf_cumsum",
    )(z, bias)


def _fox_prompt_kernel(q_ref, k_ref, v_ref, g_ref, cm_ref, cs_ref, y_ref, kb_ref, vb_ref,
                       *, n_meta, tq, scale):
    t_len = q_ref.shape[0]
    nblk = (t_len - n_meta) // tq
    kb_ref[...] = k_ref[...].astype(BF16)
    vb_ref[...] = v_ref[...].astype(BF16)
    km = kb_ref[0:n_meta, :]
    vm = vb_ref[0:n_meta, :]
    cm = cm_ref[...]

    def emit(r0, n, acc, l):
        o = acc * (1.0 / l)
        y_ref[r0:r0 + n, :] = (o * _silu(g_ref[r0:r0 + n, :])).astype(y_ref.dtype)

    s = _dot_nt(q_ref[0:n_meta, :], km) * scale - cm
    s = jnp.where(_iota2(s.shape, 1) <= _iota2(s.shape, 0), s, -jnp.inf)
    m = jnp.max(s, axis=1, keepdims=True)
    p = jnp.exp(s - m)
    emit(0, n_meta, _dot(p.astype(BF16), vm), jnp.sum(p, axis=1, keepdims=True))

    for i in range(nblk):
        r0 = n_meta + i * tq
        q = q_ref[r0:r0 + tq, :]
        s0 = _dot_nt(q, km) * scale - cm
        m_i = jnp.max(s0, axis=1, keepdims=True)
        p0 = jnp.exp(s0 - m_i)
        l_i = jnp.sum(p0, axis=1, keepdims=True)
        acc = _dot(p0.astype(BF16), vm)
        for j in range(i + 1):
            k0 = n_meta + j * tq
            s_ = _dot_nt(q, kb_ref[k0:k0 + tq, :]) * scale - cs_ref[:, j * tq:(j + 1) * tq]
            if j == i:
                s_ = jnp.where(_iota2(s_.shape, 1) <= _iota2(s_.shape, 0), s_, -jnp.inf)
            m_n = jnp.maximum(m_i, jnp.max(s_, axis=1, keepdims=True))
            alpha = jnp.exp(m_i - m_n)
            p_ = jnp.exp(s_ - m_n)
            l_i = alpha * l_i + jnp.sum(p_, axis=1, keepdims=True)
            acc = alpha * acc + _dot(p_.astype(BF16), vb_ref[k0:k0 + tq, :])
            m_i = m_n
        emit(r0, tq, acc, l_i)


def _fox_prompt(q, k_all, v_all, g, c_meta, c_seq, layer, n_heads):
    b, t, w = q.shape
    dh = w // n_heads
    s_len = t - N_META
    tq = _pick_tile(s_len, 256, LANES)
    blk = lambda: pl.BlockSpec((None, t, dh), lambda bi, hi: (bi, 0, hi))
    lblk = lambda: pl.BlockSpec((None, None, t, dh), lambda bi, hi: (layer, bi, 0, hi))
    return pl.pallas_call(
        functools.partial(_fox_prompt_kernel, n_meta=N_META, tq=tq, scale=dh ** -0.5),
        grid=(b, n_heads),
        in_specs=[blk(), lblk(), lblk(), blk(),
                  pl.BlockSpec((None, None, 1, N_META), lambda bi, hi: (bi, hi, 0, 0)),
                  pl.BlockSpec((None, None, 1, s_len), lambda bi, hi: (bi, hi, 0, 0))],
        out_specs=blk(),
        out_shape=jax.ShapeDtypeStruct((b, t, w), BF16),
        scratch_shapes=[pltpu.VMEM((t, dh), BF16), pltpu.VMEM((t, dh), BF16)],
        compiler_params=_cparams("arbitrary", "arbitrary"),
        name="fox_prompt_attn",
    )(q, k_all, v_all, g, c_meta, c_seq)


SAMPLE_PAGES_PER_STEP = (4, 2, 1)


def _fox_sample_kernel(pt_ref, q_ref, g_ref, kn_ref, vn_ref, cn_ref, *rest, scale, pps):
    del pt_ref
    kp_refs, vp_refs, lfp_refs = rest[:pps], rest[pps:2 * pps], rest[2 * pps:3 * pps]
    y_ref, m_ref, l_ref, acc_ref, carry_ref, wq_ref, lr_ref, neg_ref = rest[3 * pps:]
    p_idx = pl.program_id(1)
    n_steps = pl.num_programs(1)
    page, n_heads, dh = kp_refs[0].shape
    n_new = kn_ref.shape[0]
    n_groups = n_heads // SUBLANES
    grp_rows = SUBLANES * n_new
    grp_keys = SUBLANES * page

    @pl.when(p_idx == 0)
    def _():
        m_ref[...] = jnp.full_like(m_ref, -jnp.inf)
        l_ref[...] = jnp.zeros_like(l_ref)
        acc_ref[...] = jnp.zeros_like(acc_ref)
        carry_ref[...] = jnp.zeros_like(carry_ref)
        wq_ref[...] = q_ref[...].astype(BF16)
        lr_ref[...] = (_iota2(lr_ref.shape, 0) > _iota2(lr_ref.shape, 1) // SUBLANES).astype(F32)
        same_head = (_iota2(neg_ref.shape, 1) % SUBLANES) == (_iota2(neg_ref.shape, 0) // n_new)
        neg_ref[...] = jnp.where(same_head, 0.0, -jnp.inf)

    def update(g, s, vb):
        rows = slice(g * grp_rows, (g + 1) * grp_rows)
        m_i = m_ref[rows, :]
        m_n = jnp.maximum(m_i, jnp.max(s, axis=1, keepdims=True))
        alpha = jnp.exp(m_i - m_n)
        p = jnp.exp(s - m_n)
        l_ref[rows, :] = alpha * l_ref[rows, :] + jnp.sum(p, axis=1, keepdims=True)
        acc_ref[rows, :] = alpha * acc_ref[rows, :] + _dot(p.astype(BF16), vb)
        m_ref[rows, :] = m_n

    def group(ref, g):
        blk = ref[:, g * SUBLANES:(g + 1) * SUBLANES, :]
        return blk.reshape(blk.shape[0] * SUBLANES, dh).astype(BF16)

    carry = carry_ref[...]
    suffs = []
    for i in range(pps):
        lf = lfp_refs[i][...]
        suffs.append(_dot_exact(lf, lr_ref[...]) + carry)
        carry = carry + jnp.sum(lf, axis=1, keepdims=True)
    carry_ref[...] = carry
    own_head = (_iota2((SUBLANES, grp_keys), 1) % SUBLANES) == _iota2((SUBLANES, grp_keys), 0)
    for g in range(n_groups):
        wq = wq_ref[g * grp_rows:(g + 1) * grp_rows, :]
        s_parts = []
        for i in range(pps):
            sg = suffs[i][g * SUBLANES:(g + 1) * SUBLANES, :]
            bias = jnp.sum(jnp.where(own_head, sg, 0.0), axis=0, keepdims=True)
            s_parts.append(_dot_nt(wq, group(kp_refs[i], g)) * scale + (bias + neg_ref[...]))
        s_past = s_parts[0] if pps == 1 else jnp.concatenate(s_parts, axis=1)
        v_parts = [group(vp_refs[i], g) for i in range(pps)]
        update(g, s_past, v_parts[0] if pps == 1 else jnp.concatenate(v_parts, axis=0))

    @pl.when(p_idx == n_steps - 1)
    def _():
        new_keys = SUBLANES * n_new
        r_ = _iota2((grp_rows, new_keys), 0)
        c_ = _iota2((grp_rows, new_keys), 1)
        valid = ((c_ % SUBLANES) == (r_ // n_new)) & ((c_ // SUBLANES) <= (r_ % n_new))
        for g in range(n_groups):
            wq = wq_ref[g * grp_rows:(g + 1) * grp_rows, :]
            s_new = _dot_nt(wq, group(kn_ref, g)) * scale + cn_ref[g]
            update(g, jnp.where(valid, s_new, -jnp.inf), group(vn_ref, g))
        o = acc_ref[...] * (1.0 / l_ref[...])
        y_ref[...] = (o * _silu(g_ref[...])).astype(y_ref.dtype)


def _fox_sample(q, g_new, k_all, v_all, c_new, cache_k, cache_v, cache_lf_t, layer, page_table, n_heads):
    db, s_new, w = q.shape
    n_pages = page_table.shape[1]
    ps = cache_k.shape[2]
    dh = w // n_heads
    n_layers = k_all.shape[0]
    n_groups = n_heads // SUBLANES
    rows = s_new * n_heads
    to_rows = lambda a: a.reshape(db, s_new, n_heads, dh).transpose(0, 2, 1, 3).reshape(db, rows, dh)
    q_r, g_r = to_rows(q), to_rows(g_new)
    kn = k_all.reshape(n_layers, db, s_new, n_heads, dh)
    vn = v_all.reshape(n_layers, db, s_new, n_heads, dh)
    cn = -c_new.reshape(db, s_new, n_groups, SUBLANES).transpose(0, 2, 1, 3)
    cn = cn.reshape(db, n_groups, 1, s_new * SUBLANES)
    per_b = lambda: pl.BlockSpec((None, rows, dh), lambda b, p, pt: (b, 0, 0))
    new_spec = lambda: pl.BlockSpec((None, None, s_new, n_heads, dh), lambda b, p, pt: (layer, b, 0, 0, 0))
    pps = max(c for c in SAMPLE_PAGES_PER_STEP if n_pages % c == 0)

    def page_spec(i):
        return pl.BlockSpec(
            (None, None, ps, n_heads, dh),
            lambda b, p, pt: (layer, pt[b, n_pages - 1 - (p * pps + i)], 0, 0, 0))

    def lf_spec(i):
        return pl.BlockSpec(
            (None, None, n_heads, ps),
            lambda b, p, pt: (layer, pt[b, n_pages - 1 - (p * pps + i)], 0, 0))

    grid_spec = pltpu.PrefetchScalarGridSpec(
        num_scalar_prefetch=1,
        grid=(db, n_pages // pps),
        in_specs=[per_b(), per_b(), new_spec(), new_spec(),
                  pl.BlockSpec((None, n_groups, 1, s_new * SUBLANES), lambda b, p, pt: (b, 0, 0, 0))]
                 + [page_spec(i) for i in range(pps)] + [page_spec(i) for i in range(pps)]
                 + [lf_spec(i) for i in range(pps)],
        out_specs=per_b(),
        scratch_shapes=[pltpu.VMEM((rows, 1), F32), pltpu.VMEM((rows, 1), F32),
                        pltpu.VMEM((rows, dh), F32), pltpu.VMEM((n_heads, 1), F32),
                        pltpu.VMEM((rows, dh), BF16), pltpu.VMEM((ps, ps * SUBLANES), F32),
                        pltpu.VMEM((SUBLANES * s_new, ps * SUBLANES), F32)],
    )
    y = pl.pallas_call(
        functools.partial(_fox_sample_kernel, scale=dh ** -0.5, pps=pps),
        grid_spec=grid_spec,
        out_shape=jax.ShapeDtypeStruct((db, rows, dh), BF16),
        compiler_params=_cparams("arbitrary", "arbitrary"),
        name="fox_sample_attn",
    )(page_table, q_r, g_r, kn, vn, cn, *([cache_k] * pps), *([cache_v] * pps), *([cache_lf_t] * pps))
    return y.reshape(db, n_heads, s_new, dh).transpose(0, 2, 1, 3).reshape(db, s_new, w)


GLA_CHUNK = 128
GLA_DIAG = 8


def _gla_chunk(r0, c, refs, wa, q_scale):
    q_ref, k_ref, v_ref, g_ref, a_ref, bg_ref, on_ref, y_ref, s_ref = refs
    rows = pl.ds(r0, c)
    q = q_ref[rows, :] * q_scale
    k = k_ref[rows, :]
    v = v_ref[rows, :].astype(BF16)
    dk = q.shape[1]
    x = _dot(a_ref[rows, :].astype(BF16), wa) + bg_ref[...]
    la = _log_sigmoid(x) * (1.0 / GLA_GATE_TAU)
    row = _iota2((c, c), 0)
    col = _iota2((c, c), 1)
    b = _dot_exact((col <= row).astype(F32), la)
    b_last = b[c - 1:c, :]

    s_old = s_ref[...]
    o = _dot((q * jnp.exp(b)).astype(BF16), s_old.astype(BF16))

    a_mat = jnp.zeros((c, c), F32)
    diff = row - col
    for d in range(GLA_DIAG):
        if d == 0:
            w = q * k
        else:
            kd = pltpu.roll(k, d, 0)
            bd = pltpu.roll(b, d, 0)
            w = q * kd * jnp.exp(jnp.minimum(b - bd, 0.0))
        a_mat = jnp.where(diff == d, jnp.sum(w, axis=1, keepdims=True), a_mat)
    xor = row ^ col
    blk = 2 * GLA_DIAG
    while blk <= c:
        half = blk // 2
        pieces = [jnp.broadcast_to(b[s0 + half - 1:s0 + half, :], (blk, dk)) for s0 in range(0, c, blk)]
        b_mid = pieces[0] if len(pieces) == 1 else jnp.concatenate(pieces, axis=0)
        q_t = (q * jnp.exp(jnp.minimum(b - b_mid, 0.0))).astype(BF16)
        k_t = (k * jnp.exp(jnp.minimum(b_mid - b, 0.0))).astype(BF16)
        a_mat = jnp.where(xor >= half, _dot_nt(q_t, k_t), a_mat)
        blk *= 2
    a_mat = jnp.where(col <= row, a_mat, 0.0)
    o = o + _dot(a_mat.astype(BF16), v)

    k_end = (k * jnp.exp(b_last - b)).astype(BF16)
    decay = jnp.transpose(jnp.broadcast_to(jnp.exp(b_last), (8, dk)))[:, 0:1]
    s_ref[...] = decay * s_old + _dot_tn(k_end, v)

    o = o * lax.rsqrt(jnp.mean(o * o, axis=-1, keepdims=True) + EPS) * on_ref[...]
    y_ref[rows, :] = (o * _silu(g_ref[rows, :])).astype(y_ref.dtype)


def _gla_kernel(*refs, head, n_chunks, has_state, q_scale):
    if has_state:
        q_ref, k_ref, v_ref, g_ref, a_ref, wa_ref, bg_ref, on_ref, s0_ref, y_ref, so_ref, s_ref = refs
        s_ref[...] = s0_ref[...]
    else:
        q_ref, k_ref, v_ref, g_ref, a_ref, wa_ref, bg_ref, on_ref, y_ref, so_ref, s_ref = refs
        s_ref[...] = jnp.zeros_like(s_ref)
    wa = wa_ref[...].astype(BF16)
    crefs = (q_ref, k_ref, v_ref, g_ref, a_ref, bg_ref, on_ref, y_ref, s_ref)
    if head:
        _gla_chunk(0, head, crefs, wa, q_scale)

    def body(i, carry):
        _gla_chunk(pl.multiple_of(head + i * GLA_CHUNK, 16), GLA_CHUNK, crefs, wa, q_scale)
        return carry

    if n_chunks:
        lax.fori_loop(0, n_chunks, body, 0)
    so_ref[...] = s_ref[...]


def _gla(zq, zk, zv, zg, za, wa2, b_gate, onorm, state0, layer, n_heads):
    b, t, kw = zq.shape
    vw = zv.shape[2]
    dk, dv = kw // n_heads, vw // n_heads
    head = t % GLA_CHUNK
    n_chunks = t // GLA_CHUNK
    assert head in (0, 8, 16, 32, 64)
    tok = lambda last: pl.BlockSpec((None, t, last), lambda bi, hi: (bi, 0, hi))
    in_specs = [tok(dk), tok(dk), tok(dv), tok(dv),
                pl.BlockSpec((None, t, LANES), lambda bi, hi: (bi, 0, 0)),
                pl.BlockSpec((LANES, dk), lambda bi, hi: (0, hi)),
                pl.BlockSpec((1, dk), lambda bi, hi: (0, hi)),
                pl.BlockSpec((1, dv), lambda bi, hi: (0, 0))]
    args = [zq, zk, zv, zg, za, wa2, b_gate, onorm]
    if state0 is not None:
        in_specs.append(pl.BlockSpec((None, None, None, dk, dv), lambda bi, hi: (layer, bi, hi, 0, 0)))
        args.append(state0)
    return pl.pallas_call(
        functools.partial(_gla_kernel, head=head, n_chunks=n_chunks,
                          has_state=state0 is not None, q_scale=dk ** -0.5),
        grid=(b, n_heads),
        in_specs=in_specs,
        out_specs=[tok(dv), pl.BlockSpec((None, None, dk, dv), lambda bi, hi: (bi, hi, 0, 0))],
        out_shape=[jax.ShapeDtypeStruct((b, t, vw), BF16),
                   jax.ShapeDtypeStruct((b, n_heads, dk, dv), F32)],
        scratch_shapes=[pltpu.VMEM((dk, dv), F32)],
        compiler_params=_cparams("arbitrary", "arbitrary"),
        name="gla",
    )(*args)


def _pad_rows(x):
    return jnp.pad(x, ((0, LANES - x.shape[0]), (0, 0)))


def _fox_layer(h, nb, t, norm, w_in_t, b_f, w_out, layer, n_heads, kv_prev, sample_ctx):
    d = h.shape[1]
    aw = w_out.shape[1]
    n_layers = w_out.shape[0]
    k_prev, v_prev = kv_prev
    u = _rmsnorm(h, norm[layer], BF16)
    q = _proj(u, w_in_t, layer, 0, aw, BF16 if sample_ctx is None else F32, w_is_nk=True)
    k_all = _proj(u, w_in_t, layer, aw, aw, F32, w_is_nk=True, stack=(n_layers, k_prev))
    v_all = _proj(u, w_in_t, layer, 2 * aw, aw, F32, w_is_nk=True, stack=(n_layers, v_prev))
    g = _proj(u, w_in_t, layer, 3 * aw, aw, F32, w_is_nk=True)
    z_f = _tail_proj(u, _pad_rows(w_in_t[layer, 4 * aw:, :]))
    bias = jnp.pad(b_f[layer][None, :].astype(F32), ((0, 0), (0, LANES - n_heads)))
    lf, c = _logf_cumsum(z_f, bias, nb, t)
    lf = lf[:, :n_heads].reshape(nb, t, n_heads)
    c = c[:, :n_heads].reshape(nb, t, n_heads)
    shp = (nb, t, aw)
    if sample_ctx is None:
        c_t = jnp.transpose(c, (0, 2, 1))[:, :, None, :]
        y = _fox_prompt(q.reshape(shp), k_all.reshape(n_layers, nb, t, aw),
                        v_all.reshape(n_layers, nb, t, aw), g.reshape(shp),
                        c_t[..., :N_META], c_t[..., N_META:], layer, n_heads)
    else:
        cache_k, cache_v, cache_lf_t, page_table = sample_ctx
        y = _fox_sample(q.reshape(shp), g.reshape(shp), k_all, v_all, c,
                        cache_k, cache_v, cache_lf_t, layer, page_table, n_heads)
    h = _proj(y.reshape(nb * t, aw), w_out, layer, 0, d, F32, w_is_nk=False, residual=h)
    return h, k_all, v_all, lf


def _gla_layer(h, nb, t, norm, w_in_t, w_a2, b_gate, onorm, w_out, layer, n_heads, state0):
    d = h.shape[1]
    vw = w_out.shape[1]
    kw = w_a2.shape[2]
    u = _rmsnorm(h, norm[layer], BF16)
    zq = _proj(u, w_in_t, layer, 0, kw, F32, w_is_nk=True)
    zk = _proj(u, w_in_t, layer, kw, kw, F32, w_is_nk=True)
    zv = _proj(u, w_in_t, layer, 2 * kw, vw, F32, w_is_nk=True)
    zg = _proj(u, w_in_t, layer, 2 * kw + vw, vw, F32, w_is_nk=True)
    za = _tail_proj(u, _pad_rows(w_in_t[layer, 2 * kw + 2 * vw:, :]))
    wa2 = _pad_rows(w_a2[layer])
    y, s_new = _gla(zq.reshape(nb, t, kw), zk.reshape(nb, t, kw), zv.reshape(nb, t, vw),
                    zg.reshape(nb, t, vw), za.reshape(nb, t, LANES), wa2,
                    b_gate[layer][None, :], onorm[layer][None, :], state0, layer, n_heads)
    h = _proj(y.reshape(nb * t, vw), w_out, layer, 0, d, F32, w_is_nk=False, residual=h)
    return h, s_new


def kernel(x_prompt, x_sample, cache_k, cache_v, cache_logf, state_gla, page_table, meta_tokens,
           norm_a, w_in_a, b_forget, w_out_a, norm_b, w_in_b, w_gate_up, b_gate, onorm_b, w_out_b,
           final_norm):
    bsz, seq, d = x_prompt.shape
    db, ds, _ = x_sample.shape
    a_heads = b_forget.shape[1]
    b_heads = state_gla.shape[2]
    dh = w_out_a.shape[1] // a_heads
    depth = norm_a.shape[0] + norm_b.shape[0]
    t = N_META + seq
    meta = jnp.broadcast_to(meta_tokens.astype(x_prompt.dtype)[None], (bsz, N_META, d))
    h_p = jnp.concatenate([meta, x_prompt], axis=1).reshape(bsz * t, d)
    h_s = x_sample.reshape(db * ds, d)
    w_in_a_t = jnp.swapaxes(w_in_a, 1, 2)
    w_in_b_t = jnp.swapaxes(w_in_b, 1, 2)
    cache_lf_t = jnp.swapaxes(cache_logf, 2, 3)
    kv_p, kv_s = (None, None), (None, None)
    nf_p, nf_s, ns_p, ns_s = [], [], [], []
    for i in range(depth):
        j = i // 2
        if i % 2 == 0:
            h_p, k, v, lf = _fox_layer(h_p, bsz, t, norm_a, w_in_a_t, b_forget, w_out_a, j, a_heads,
                                       kv_p, None)
            kv_p = (k, v)
            nf_p.append(lf)
            h_s, k, v, lf = _fox_layer(h_s, db, ds, norm_a, w_in_a_t, b_forget, w_out_a, j, a_heads,
                                       kv_s, (cache_k, cache_v, cache_lf_t, page_table))
            kv_s = (k, v)
            nf_s.append(lf)
        else:
            h_p, s_p = _gla_layer(h_p, bsz, t, norm_b, w_in_b_t, w_gate_up, b_gate, onorm_b, w_out_b,
                                  j, b_heads, None)
            ns_p.append(s_p)
            h_s, s_s = _gla_layer(h_s, db, ds, norm_b, w_in_b_t, w_gate_up, b_gate, onorm_b, w_out_b,
                                  j, b_heads, state_gla)
            ns_s.append(s_s)
    y_prompt = _rmsnorm(h_p, final_norm, F32).reshape(bsz, t, d)[:, N_META:]
    y_sample = _rmsnorm(h_s, final_norm, F32).reshape(db, ds, d)
    n_a = norm_a.shape[0]
    kp_shape = (n_a, bsz, t, a_heads, dh)
    ks_shape = (n_a, db, ds, a_heads, dh)
    return (y_prompt, y_sample, kv_p[0].reshape(kp_shape), kv_p[1].reshape(kp_shape), jnp.stack(nf_p),
            kv_s[0].reshape(ks_shape), kv_s[1].reshape(ks_shape), jnp.stack(nf_s),
            jnp.stack(ns_p), jnp.stack(ns_s))
```

```python
import functools

import jax
import jax.numpy as jnp
from jax import lax
from jax.experimental import pallas as pl
from jax.experimental.pallas import tpu as pltpu

EPS = 1e-6
N_META = 16
GLA_GATE_TAU = 16.0
LOG2E = 1.4426950408889634
LANES = 128
SUBLANES = 8
VMEM_LIMIT_BYTES = 56 * 1024 * 1024
F32 = jnp.float32
BF16 = jnp.bfloat16
HIGHEST = lax.Precision.HIGHEST


def _cparams(*sem):
    return pltpu.CompilerParams(dimension_semantics=sem, vmem_limit_bytes=VMEM_LIMIT_BYTES)


def _pick_tile(n, target, mult):
    best = None
    for t in range(mult, min(n, target) + 1, mult):
        if n % t == 0:
            best = t
    return best if best is not None else n


def _dot(a, b):
    return jnp.dot(a, b, preferred_element_type=F32)


def _dot_nt(a, b, precision=None):
    return lax.dot_general(a, b, (((1,), (1,)), ((), ())), precision=precision,
                           preferred_element_type=F32)


def _dot_tn(a, b):
    return lax.dot_general(a, b, (((0,), (0,)), ((), ())), preferred_element_type=F32)


def _dot_exact(a, b):
    return jnp.dot(a, b, precision=HIGHEST, preferred_element_type=F32)


def _log_sigmoid(x):
    return jnp.minimum(x, 0.0) - jnp.log1p(jnp.exp(-jnp.abs(x)))


def _silu(x):
    return x * (1.0 / (1.0 + jnp.exp(-x)))


def _iota2(shape, dim):
    return lax.broadcasted_iota(jnp.int32, shape, dim)


def _rmsnorm_kernel(x_ref, g_ref, o_ref):
    x = x_ref[...]
    ms = jnp.mean(x * x, axis=-1, keepdims=True)
    o_ref[...] = (x * lax.rsqrt(ms + EPS) * g_ref[...]).astype(o_ref.dtype)


def _rmsnorm(x, g, out_dtype):
    m, d = x.shape
    tm = _pick_tile(m, 512, 16)
    return pl.pallas_call(
        _rmsnorm_kernel,
        grid=(m // tm,),
        in_specs=[pl.BlockSpec((tm, d), lambda i: (i, 0)),
                  pl.BlockSpec((1, d), lambda i: (0, 0))],
        out_specs=pl.BlockSpec((tm, d), lambda i: (i, 0)),
        out_shape=jax.ShapeDtypeStruct((m, d), out_dtype),
        compiler_params=_cparams("arbitrary"),
        name="rmsnorm",
    )(x, g.reshape(1, d))


def _proj_kernel(*refs, w_is_nk, has_res, n_prev, out_scale):
    x_ref, xs_ref, w_ref = refs[:3]
    r_ref, rs_ref = refs[3:5] if has_res else (None, None)
    o_ref, os_ref, wb_ref = refs[(5 if has_res else 3) + n_prev:]

    def mm(x):
        return _dot_nt(x, wb_ref[...]) if w_is_nk else _dot(x, wb_ref[...])

    @pl.when(pl.program_id(1) == 0)
    def _():
        wb_ref[...] = w_ref[...].astype(BF16)
        acc_s = mm(xs_ref[...])
        if has_res:
            acc_s = rs_ref[...] + acc_s
        os_ref[...] = acc_s.astype(os_ref.dtype)

    acc = mm(x_ref[...])
    if has_res:
        acc = r_ref[...] + acc
    if out_scale is not None:
        acc = acc * out_scale
    o_ref[...] = acc.astype(o_ref.dtype)


def _proj(x, xs, w, layer, col0, ncols, out_dtype, *, w_is_nk, residual=None, stack=None,
          out_scale=None, tn=512, tm_target=704):
    m, k = x.shape
    ms = xs.shape[0]
    tm = _pick_tile(m, tm_target, 16)
    assert ncols % tn == 0 and col0 % tn == 0
    cb = col0 // tn
    if w_is_nk:
        w_spec = pl.BlockSpec((None, tn, k), lambda n, i: (layer, cb + n, 0))
        wb_shape = (tn, k)
    else:
        w_spec = pl.BlockSpec((None, k, tn), lambda n, i: (layer, 0, cb + n))
        wb_shape = (k, tn)
    in_specs = [pl.BlockSpec((tm, k), lambda n, i: (i, 0)),
                pl.BlockSpec((ms, k), lambda n, i: (0, 0)), w_spec]
    args = [x, xs, w]
    aliases = {}
    if residual is not None:
        in_specs += [pl.BlockSpec((tm, tn), lambda n, i: (i, n)),
                     pl.BlockSpec((ms, tn), lambda n, i: (0, n))]
        args += list(residual)
        aliases = {3: 0, 4: 1}
    n_prev = 0
    if stack is None:
        out_specs = [pl.BlockSpec((tm, tn), lambda n, i: (i, n)),
                     pl.BlockSpec((ms, tn), lambda n, i: (0, n))]
        out_shape = [jax.ShapeDtypeStruct((m, ncols), out_dtype),
                     jax.ShapeDtypeStruct((ms, ncols), F32)]
    else:
        n_layers, prev, prev_s = stack
        out_specs = [pl.BlockSpec((None, tm, tn), lambda n, i: (layer, i, n)),
                     pl.BlockSpec((None, ms, tn), lambda n, i: (layer, 0, n))]
        out_shape = [jax.ShapeDtypeStruct((n_layers, m, ncols), out_dtype),
                     jax.ShapeDtypeStruct((n_layers, ms, ncols), F32)]
        if prev is not None:
            n_prev = 2
            in_specs += [pl.BlockSpec(memory_space=pl.ANY)] * 2
            aliases = {len(args): 0, len(args) + 1: 1}
            args += [prev, prev_s]
    return pl.pallas_call(
        functools.partial(_proj_kernel, w_is_nk=w_is_nk, has_res=residual is not None,
                          n_prev=n_prev, out_scale=out_scale),
        grid=(ncols // tn, m // tm),
        in_specs=in_specs,
        out_specs=out_specs,
        out_shape=out_shape,
        scratch_shapes=[pltpu.VMEM(wb_shape, BF16)],
        input_output_aliases=aliases,
        compiler_params=_cparams("arbitrary", "arbitrary"),
        name="proj",
    )(*args)


def _tail_kernel(x_ref, w_ref, o_ref):
    o_ref[...] = _dot_nt(x_ref[...], w_ref[...].astype(BF16))


def _tail_proj(x, w_tail):
    m, k = x.shape
    tm = _pick_tile(m, 704, 16)
    return pl.pallas_call(
        _tail_kernel,
        grid=(m // tm,),
        in_specs=[pl.BlockSpec((tm, k), lambda i: (i, 0)),
                  pl.BlockSpec((LANES, k), lambda i: (0, 0))],
        out_specs=pl.BlockSpec((tm, LANES), lambda i: (i, 0)),
        out_shape=jax.ShapeDtypeStruct((m, LANES), F32),
        compiler_params=_cparams("arbitrary"),
        name="tail_proj",
    )(x, w_tail)


def _logf_kernel(z_ref, b_ref, lf_ref, c_ref, carry_ref, *, seg):
    tr = z_ref.shape[0]
    lf = _log_sigmoid(z_ref[...] + b_ref[...])
    lf_ref[...] = lf
    row = _iota2((tr, tr), 0)
    col = _iota2((tr, tr), 1)
    if seg >= tr:
        @pl.when(pl.program_id(1) == 0)
        def _():
            carry_ref[...] = jnp.zeros_like(carry_ref)

        tri = (col <= row).astype(F32)
        c = _dot_exact(tri, lf) + carry_ref[...]
        c_ref[...] = c
        carry_ref[...] = c[tr - 1:tr, :]
    else:
        tri = ((col <= row) & ((row // seg) == (col // seg))).astype(F32)
        c_ref[...] = _dot_exact(tri, lf)


def _logf_cumsum(z, bias, nb, t):
    m = nb * t
    if t >= 128:
        tr = _pick_tile(t, 704, 8)
        grid = (nb, t // tr)
        spec = pl.BlockSpec((tr, LANES), lambda b, i: (b * (t // tr) + i, 0))
    else:
        tr = m
        grid = (1, 1)
        spec = pl.BlockSpec((tr, LANES), lambda b, i: (0, 0))
    return pl.pallas_call(
        functools.partial(_logf_kernel, seg=t),
        grid=grid,
        in_specs=[spec, pl.BlockSpec((1, LANES), lambda b, i: (0, 0))],
        out_specs=[spec, spec],
        out_shape=[jax.ShapeDtypeStruct((m, LANES), F32)] * 2,
        scratch_shapes=[pltpu.VMEM((1, LANES), F32)],
        compiler_params=_cparams("arbitrary", "arbitrary"),
        name="logf_cumsum",
    )(z, bias)


def _fox_prompt_kernel(q_ref, k_ref, v_ref, g_ref, cm_ref, cs_ref, y_ref, kb_ref, vb_ref,
                       *, n_meta, tq):
    t_len = q_ref.shape[0]
    nblk = (t_len - n_meta) // tq
    kb_ref[...] = k_ref[...].astype(BF16)
    vb_ref[...] = v_ref[...].astype(BF16)
    km = kb_ref[0:n_meta, :]
    vm = vb_ref[0:n_meta, :]
    cm = cm_ref[...] * LOG2E
    cs = cs_ref[...] * LOG2E

    def emit(r0, n, acc, l):
        o = acc * (1.0 / l)
        y_ref[r0:r0 + n, :] = (o * _silu(g_ref[r0:r0 + n, :])).astype(y_ref.dtype)

    s = _dot_nt(q_ref[0:n_meta, :], km) - cm
    s = jnp.where(_iota2(s.shape, 1) <= _iota2(s.shape, 0), s, -jnp.inf)
    m = jnp.max(s, axis=1, keepdims=True)
    p = jnp.exp2(s - m)
    emit(0, n_meta, _dot(p.astype(BF16), vm), jnp.sum(p, axis=1, keepdims=True))

    for i in range(nblk):
        r0 = n_meta + i * tq
        q = q_ref[r0:r0 + tq, :]
        s0 = _dot_nt(q, km) - cm
        m_i = jnp.max(s0, axis=1, keepdims=True)
        p0 = jnp.exp2(s0 - m_i)
        l_i = jnp.sum(p0, axis=1, keepdims=True)
        acc = _dot(p0.astype(BF16), vm)
        for j in range(i + 1):
            k0 = n_meta + j * tq
            s_ = _dot_nt(q, kb_ref[k0:k0 + tq, :]) - cs[:, j * tq:(j + 1) * tq]
            if j == i:
                s_ = jnp.where(_iota2(s_.shape, 1) <= _iota2(s_.shape, 0), s_, -jnp.inf)
            m_n = jnp.maximum(m_i, jnp.max(s_, axis=1, keepdims=True))
            alpha = jnp.exp2(m_i - m_n)
            p_ = jnp.exp2(s_ - m_n)
            l_i = alpha * l_i + jnp.sum(p_, axis=1, keepdims=True)
            acc = alpha * acc + _dot(p_.astype(BF16), vb_ref[k0:k0 + tq, :])
            m_i = m_n
        emit(r0, tq, acc, l_i)


def _fox_prompt(q, k_all, v_all, g, c_meta, c_seq, layer, n_heads):
    b, t, w = q.shape
    dh = w // n_heads
    s_len = t - N_META
    tq = _pick_tile(s_len, 256, LANES)
    blk = lambda: pl.BlockSpec((None, t, dh), lambda bi, hi: (bi, 0, hi))
    lblk = lambda: pl.BlockSpec((None, None, t, dh), lambda bi, hi: (layer, bi, 0, hi))
    return pl.pallas_call(
        functools.partial(_fox_prompt_kernel, n_meta=N_META, tq=tq),
        grid=(b, n_heads),
        in_specs=[blk(), lblk(), lblk(), blk(),
                  pl.BlockSpec((None, None, 1, N_META), lambda bi, hi: (bi, hi, 0, 0)),
                  pl.BlockSpec((None, None, 1, s_len), lambda bi, hi: (bi, hi, 0, 0))],
        out_specs=blk(),
        out_shape=jax.ShapeDtypeStruct((b, t, w), BF16),
        scratch_shapes=[pltpu.VMEM((t, dh), BF16), pltpu.VMEM((t, dh), BF16)],
        compiler_params=_cparams("arbitrary", "arbitrary"),
        name="fox_prompt_attn",
    )(q, k_all, v_all, g, c_meta, c_seq)


SAMPLE_PAGES_PER_STEP = (4, 2, 1)


def _fox_sample_kernel(pt_ref, q_ref, g_ref, kn_ref, vn_ref, cn_ref, *rest, scale, pps):
    del pt_ref
    kp_refs, vp_refs, lfp_refs = rest[:pps], rest[pps:2 * pps], rest[2 * pps:3 * pps]
    y_ref, m_ref, l_ref, acc_ref, carry_ref, wq_ref, lr_ref, neg_ref = rest[3 * pps:]
    p_idx = pl.program_id(1)
    n_steps = pl.num_programs(1)
    page, n_heads, dh = kp_refs[0].shape
    n_new = kn_ref.shape[0]
    n_groups = n_heads // SUBLANES
    grp_rows = SUBLANES * n_new
    grp_keys = SUBLANES * page

    @pl.when(p_idx == 0)
    def _():
        m_ref[...] = jnp.full_like(m_ref, -jnp.inf)
        l_ref[...] = jnp.zeros_like(l_ref)
        acc_ref[...] = jnp.zeros_like(acc_ref)
        carry_ref[...] = jnp.zeros_like(carry_ref)
        wq_ref[...] = q_ref[...].astype(BF16)
        lr_ref[...] = (_iota2(lr_ref.shape, 0) > _iota2(lr_ref.shape, 1) // SUBLANES).astype(F32)
        same_head = (_iota2(neg_ref.shape, 1) % SUBLANES) == (_iota2(neg_ref.shape, 0) // n_new)
        neg_ref[...] = jnp.where(same_head, 0.0, -jnp.inf)

    def update(g, s, vb):
        rows = slice(g * grp_rows, (g + 1) * grp_rows)
        m_i = m_ref[rows, :]
        m_n = jnp.maximum(m_i, jnp.max(s, axis=1, keepdims=True))
        alpha = jnp.exp(m_i - m_n)
        p = jnp.exp(s - m_n)
        l_ref[rows, :] = alpha * l_ref[rows, :] + jnp.sum(p, axis=1, keepdims=True)
        acc_ref[rows, :] = alpha * acc_ref[rows, :] + _dot(p.astype(BF16), vb)
        m_ref[rows, :] = m_n

    def group(ref, g):
        blk = ref[:, g * SUBLANES:(g + 1) * SUBLANES, :]
        return blk.reshape(blk.shape[0] * SUBLANES, dh).astype(BF16)

    carry = carry_ref[...]
    suffs = []
    for i in range(pps):
        lf = lfp_refs[i][...]
        suffs.append(_dot_exact(lf, lr_ref[...]) + carry)
        carry = carry + jnp.sum(lf, axis=1, keepdims=True)
    carry_ref[...] = carry
    own_head = (_iota2((SUBLANES, grp_keys), 1) % SUBLANES) == _iota2((SUBLANES, grp_keys), 0)
    for g in range(n_groups):
        wq = wq_ref[g * grp_rows:(g + 1) * grp_rows, :]
        s_parts = []
        for i in range(pps):
            sg = suffs[i][g * SUBLANES:(g + 1) * SUBLANES, :]
            bias = jnp.sum(jnp.where(own_head, sg, 0.0), axis=0, keepdims=True)
            s_parts.append(_dot_nt(wq, group(kp_refs[i], g)) * scale + (bias + neg_ref[...]))
        s_past = s_parts[0] if pps == 1 else jnp.concatenate(s_parts, axis=1)
        v_parts = [group(vp_refs[i], g) for i in range(pps)]
        update(g, s_past, v_parts[0] if pps == 1 else jnp.concatenate(v_parts, axis=0))

    @pl.when(p_idx == n_steps - 1)
    def _():
        new_keys = SUBLANES * n_new
        r_ = _iota2((grp_rows, new_keys), 0)
        c_ = _iota2((grp_rows, new_keys), 1)
        valid = ((c_ % SUBLANES) == (r_ // n_new)) & ((c_ // SUBLANES) <= (r_ % n_new))
        for g in range(n_groups):
            wq = wq_ref[g * grp_rows:(g + 1) * grp_rows, :]
            s_new = _dot_nt(wq, group(kn_ref, g)) * scale + cn_ref[g]
            update(g, jnp.where(valid, s_new, -jnp.inf), group(vn_ref, g))
        o = acc_ref[...] * (1.0 / l_ref[...])
        y_ref[...] = (o * _silu(g_ref[...])).astype(y_ref.dtype)


def _fox_sample(q, g_new, k_all, v_all, c_new, cache_k, cache_v, cache_lf_t, layer, page_table, n_heads):
    db, s_new, w = q.shape
    n_pages = page_table.shape[1]
    ps = cache_k.shape[2]
    dh = w // n_heads
    n_layers = k_all.shape[0]
    n_groups = n_heads // SUBLANES
    rows = s_new * n_heads
    to_rows = lambda a: a.reshape(db, s_new, n_heads, dh).transpose(0, 2, 1, 3).reshape(db, rows, dh)
    q_r, g_r = to_rows(q), to_rows(g_new)
    kn = k_all.reshape(n_layers, db, s_new, n_heads, dh)
    vn = v_all.reshape(n_layers, db, s_new, n_heads, dh)
    cn = -c_new.reshape(db, s_new, n_groups, SUBLANES).transpose(0, 2, 1, 3)
    cn = cn.reshape(db, n_groups, 1, s_new * SUBLANES)
    per_b = lambda: pl.BlockSpec((None, rows, dh), lambda b, p, pt: (b, 0, 0))
    new_spec = lambda: pl.BlockSpec((None, None, s_new, n_heads, dh), lambda b, p, pt: (layer, b, 0, 0, 0))
    pps = max(c for c in SAMPLE_PAGES_PER_STEP if n_pages % c == 0)

    def page_spec(i):
        return pl.BlockSpec(
            (None, None, ps, n_heads, dh),
            lambda b, p, pt: (layer, pt[b, n_pages - 1 - (p * pps + i)], 0, 0, 0))

    def lf_spec(i):
        return pl.BlockSpec(
            (None, None, n_heads, ps),
            lambda b, p, pt: (layer, pt[b, n_pages - 1 - (p * pps + i)], 0, 0))

    grid_spec = pltpu.PrefetchScalarGridSpec(
        num_scalar_prefetch=1,
        grid=(db, n_pages // pps),
        in_specs=[per_b(), per_b(), new_spec(), new_spec(),
                  pl.BlockSpec((None, n_groups, 1, s_new * SUBLANES), lambda b, p, pt: (b, 0, 0, 0))]
                 + [page_spec(i) for i in range(pps)] + [page_spec(i) for i in range(pps)]
                 + [lf_spec(i) for i in range(pps)],
        out_specs=per_b(),
        scratch_shapes=[pltpu.VMEM((rows, 1), F32), pltpu.VMEM((rows, 1), F32),
                        pltpu.VMEM((rows, dh), F32), pltpu.VMEM((n_heads, 1), F32),
                        pltpu.VMEM((rows, dh), BF16), pltpu.VMEM((ps, ps * SUBLANES), F32),
                        pltpu.VMEM((SUBLANES * s_new, ps * SUBLANES), F32)],
    )
    y = pl.pallas_call(
        functools.partial(_fox_sample_kernel, scale=dh ** -0.5, pps=pps),
        grid_spec=grid_spec,
        out_shape=jax.ShapeDtypeStruct((db, rows, dh), BF16),
        compiler_params=_cparams("arbitrary", "arbitrary"),
        name="fox_sample_attn",
    )(page_table, q_r, g_r, kn, vn, cn, *([cache_k] * pps), *([cache_v] * pps), *([cache_lf_t] * pps))
    return y.reshape(db, n_heads, s_new, dh).transpose(0, 2, 1, 3).reshape(db, s_new, w)


GLA_CHUNK = 128
GLA_DIAG = 8


def _gla_chunk(r0, c, refs, wa, q_scale):
    q_ref, k_ref, v_ref, g_ref, a_ref, bg_ref, on_ref, y_ref, s_ref = refs
    rows = pl.ds(r0, c)
    q = q_ref[rows, :] * q_scale
    k = k_ref[rows, :]
    v = v_ref[rows, :].astype(BF16)
    dk = q.shape[1]
    x = _dot(a_ref[rows, :].astype(BF16), wa) + bg_ref[...]
    la = _log_sigmoid(x) * (1.0 / GLA_GATE_TAU)
    row = _iota2((c, c), 0)
    col = _iota2((c, c), 1)
    b = _dot_exact((col <= row).astype(F32), la)
    b_last = b[c - 1:c, :]

    s_old = s_ref[...]
    o = _dot((q * jnp.exp(b)).astype(BF16), s_old.astype(BF16))

    a_mat = jnp.zeros((c, c), F32)
    diff = row - col
    for d in range(GLA_DIAG):
        if d == 0:
            w = q * k
        else:
            kd = pltpu.roll(k, d, 0)
            bd = pltpu.roll(b, d, 0)
            w = q * kd * jnp.exp(jnp.minimum(b - bd, 0.0))
        a_mat = jnp.where(diff == d, jnp.sum(w, axis=1, keepdims=True), a_mat)
    xor = row ^ col
    blk = 2 * GLA_DIAG
    while blk <= c:
        half = blk // 2
        pieces = [jnp.broadcast_to(b[s0 + half - 1:s0 + half, :], (blk, dk)) for s0 in range(0, c, blk)]
        b_mid = pieces[0] if len(pieces) == 1 else jnp.concatenate(pieces, axis=0)
        q_t = (q * jnp.exp(jnp.minimum(b - b_mid, 0.0))).astype(BF16)
        k_t = (k * jnp.exp(jnp.minimum(b_mid - b, 0.0))).astype(BF16)
        a_mat = jnp.where(xor >= half, _dot_nt(q_t, k_t), a_mat)
        blk *= 2
    a_mat = jnp.where(col <= row, a_mat, 0.0)
    o = o + _dot(a_mat.astype(BF16), v)

    k_end = (k * jnp.exp(b_last - b)).astype(BF16)
    decay = jnp.transpose(jnp.broadcast_to(jnp.exp(b_last), (8, dk)))[:, 0:1]
    s_ref[...] = decay * s_old + _dot_tn(k_end, v)

    o = o * lax.rsqrt(jnp.mean(o * o, axis=-1, keepdims=True) + EPS) * on_ref[...]
    y_ref[rows, :] = (o * _silu(g_ref[rows, :])).astype(y_ref.dtype)


def _gla_kernel(*refs, head, n_chunks, has_state, has_prev, q_scale):
    q_ref, k_ref, v_ref, g_ref, a_ref, wa_ref, bg_ref, on_ref = refs[:8]
    y_ref, so_ref, s_ref = refs[8 + has_state + has_prev:]
    if has_state:
        s_ref[...] = refs[8][...]
    else:
        s_ref[...] = jnp.zeros_like(s_ref)
    wa = wa_ref[...].astype(BF16)
    crefs = (q_ref, k_ref, v_ref, g_ref, a_ref, bg_ref, on_ref, y_ref, s_ref)
    if head:
        _gla_chunk(0, head, crefs, wa, q_scale)

    def body(i, carry):
        _gla_chunk(pl.multiple_of(head + i * GLA_CHUNK, 16), GLA_CHUNK, crefs, wa, q_scale)
        return carry

    if n_chunks:
        lax.fori_loop(0, n_chunks, body, 0)
    so_ref[...] = s_ref[...]


def _gla(zq, zk, zv, zg, za, wa2, b_gate, onorm, state0, layer, n_layers, s_prev, n_heads):
    b, t, kw = zq.shape
    vw = zv.shape[2]
    dk, dv = kw // n_heads, vw // n_heads
    head = t % GLA_CHUNK
    n_chunks = t // GLA_CHUNK
    assert head in (0, 8, 16, 32, 64)
    tok = lambda last: pl.BlockSpec((None, t, last), lambda bi, hi: (bi, 0, hi))
    in_specs = [tok(dk), tok(dk), tok(dv), tok(dv),
                pl.BlockSpec((None, t, LANES), lambda bi, hi: (bi, 0, 0)),
                pl.BlockSpec((LANES, dk), lambda bi, hi: (0, hi)),
                pl.BlockSpec((1, dk), lambda bi, hi: (0, hi)),
                pl.BlockSpec((1, dv), lambda bi, hi: (0, 0))]
    args = [zq, zk, zv, zg, za, wa2, b_gate, onorm]
    if state0 is not None:
        in_specs.append(pl.BlockSpec((None, None, None, dk, dv), lambda bi, hi: (layer, bi, hi, 0, 0)))
        args.append(state0)
    aliases = {}
    if s_prev is not None:
        in_specs.append(pl.BlockSpec(memory_space=pl.ANY))
        aliases = {len(args): 1}
        args.append(s_prev)
    return pl.pallas_call(
        functools.partial(_gla_kernel, head=head, n_chunks=n_chunks, has_state=state0 is not None,
                          has_prev=s_prev is not None, q_scale=dk ** -0.5),
        grid=(b, n_heads),
        in_specs=in_specs,
        out_specs=[tok(dv),
                   pl.BlockSpec((None, None, None, dk, dv), lambda bi, hi: (layer, bi, hi, 0, 0))],
        out_shape=[jax.ShapeDtypeStruct((b, t, vw), BF16),
                   jax.ShapeDtypeStruct((n_layers, b, n_heads, dk, dv), F32)],
        scratch_shapes=[pltpu.VMEM((dk, dv), F32)],
        input_output_aliases=aliases,
        compiler_params=_cparams("arbitrary", "arbitrary"),
        name="gla",
    )(*args)


def _pad_rows(x):
    return jnp.pad(x, ((0, LANES - x.shape[0]), (0, 0)))


def _fox_layer(h_p, h_s, dims, norm, w_in_t, b_f, w_out, layer, n_heads, prev, sample_ctx):
    bsz, t, db, ds = dims
    d = h_p.shape[1]
    aw = w_out.shape[1]
    n_layers = w_out.shape[0]
    dh = aw // n_heads
    kp_prev, vp_prev, ks_prev, vs_prev = prev
    u_p = _rmsnorm(h_p, norm[layer], BF16)
    u_s = _rmsnorm(h_s, norm[layer], BF16)
    proj_in = functools.partial(_proj, u_p, u_s, w_in_t, layer, w_is_nk=True)
    q_p, q_s = proj_in(0, aw, BF16, out_scale=dh ** -0.5 * LOG2E)
    k_p, k_s = proj_in(aw, aw, F32, stack=(n_layers, kp_prev, ks_prev))
    v_p, v_s = proj_in(2 * aw, aw, F32, stack=(n_layers, vp_prev, vs_prev))
    g_p, g_s = proj_in(3 * aw, aw, F32)
    w_tail = _pad_rows(w_in_t[layer, 4 * aw:, :])
    bias = jnp.pad(b_f[layer][None, :].astype(F32), ((0, 0), (0, LANES - n_heads)))
    lf_p, c_p = _logf_cumsum(_tail_proj(u_p, w_tail), bias, bsz, t)
    lf_s, c_s = _logf_cumsum(_tail_proj(u_s, w_tail), bias, db, ds)
    lf_p = lf_p[:, :n_heads].reshape(bsz, t, n_heads)
    lf_s = lf_s[:, :n_heads].reshape(db, ds, n_heads)
    c_t = jnp.transpose(c_p[:, :n_heads].reshape(bsz, t, n_heads), (0, 2, 1))[:, :, None, :]
    y_p = _fox_prompt(q_p.reshape(bsz, t, aw), k_p.reshape(n_layers, bsz, t, aw),
                      v_p.reshape(n_layers, bsz, t, aw), g_p.reshape(bsz, t, aw),
                      c_t[..., :N_META], c_t[..., N_META:], layer, n_heads)
    cache_k, cache_v, cache_lf_t, page_table = sample_ctx
    y_s = _fox_sample(q_s.reshape(db, ds, aw), g_s.reshape(db, ds, aw), k_s, v_s,
                      c_s[:, :n_heads].reshape(db, ds, n_heads),
                      cache_k, cache_v, cache_lf_t, layer, page_table, n_heads)
    h_p, h_s = _proj(y_p.reshape(bsz * t, aw), y_s.reshape(db * ds, aw), w_out, layer, 0, d, F32,
                     w_is_nk=False, residual=(h_p, h_s))
    return h_p, h_s, (k_p, v_p, k_s, v_s), lf_p, lf_s


def _gla_layer(h_p, h_s, dims, norm, w_in_t, w_a2, b_gate, onorm, w_out, layer, n_heads, state0, prev):
    bsz, t, db, ds = dims
    d = h_p.shape[1]
    vw = w_out.shape[1]
    kw = w_a2.shape[2]
    n_layers = w_out.shape[0]
    u_p = _rmsnorm(h_p, norm[layer], BF16)
    u_s = _rmsnorm(h_s, norm[layer], BF16)
    proj_in = functools.partial(_proj, u_p, u_s, w_in_t, layer, w_is_nk=True)
    zq_p, zq_s = proj_in(0, kw, F32)
    zk_p, zk_s = proj_in(kw, kw, F32)
    zv_p, zv_s = proj_in(2 * kw, vw, F32)
    zg_p, zg_s = proj_in(2 * kw + vw, vw, F32)
    w_tail = _pad_rows(w_in_t[layer, 2 * kw + 2 * vw:, :])
    za_p, za_s = _tail_proj(u_p, w_tail), _tail_proj(u_s, w_tail)
    wa2 = _pad_rows(w_a2[layer])
    bg, on = b_gate[layer][None, :], onorm[layer][None, :]
    y_p, st_p = _gla(zq_p.reshape(bsz, t, kw), zk_p.reshape(bsz, t, kw), zv_p.reshape(bsz, t, vw),
                     zg_p.reshape(bsz, t, vw), za_p.reshape(bsz, t, LANES), wa2, bg, on,
                     None, layer, n_layers, prev[0], n_heads)
    y_s, st_s = _gla(zq_s.reshape(db, ds, kw), zk_s.reshape(db, ds, kw), zv_s.reshape(db, ds, vw),
                     zg_s.reshape(db, ds, vw), za_s.reshape(db, ds, LANES), wa2, bg, on,
                     state0, layer, n_layers, prev[1], n_heads)
    h_p, h_s = _proj(y_p.reshape(bsz * t, vw), y_s.reshape(db * ds, vw), w_out, layer, 0, d, F32,
                     w_is_nk=False, residual=(h_p, h_s))
    return h_p, h_s, (st_p, st_s)


def kernel(x_prompt, x_sample, cache_k, cache_v, cache_logf, state_gla, page_table, meta_tokens,
           norm_a, w_in_a, b_forget, w_out_a, norm_b, w_in_b, w_gate_up, b_gate, onorm_b, w_out_b,
           final_norm):
    bsz, seq, d = x_prompt.shape
    db, ds, _ = x_sample.shape
    a_heads = b_forget.shape[1]
    b_heads = state_gla.shape[2]
    dh = w_out_a.shape[1] // a_heads
    depth = norm_a.shape[0] + norm_b.shape[0]
    t = N_META + seq
    meta = jnp.broadcast_to(meta_tokens.astype(x_prompt.dtype)[None], (bsz, N_META, d))
    h_p = jnp.concatenate([meta, x_prompt], axis=1).reshape(bsz * t, d)
    h_s = x_sample.reshape(db * ds, d)
    w_in_a_t = jnp.swapaxes(w_in_a, 1, 2)
    w_in_b_t = jnp.swapaxes(w_in_b, 1, 2)
    cache_lf_t = jnp.swapaxes(cache_logf, 2, 3)
    dims = (bsz, t, db, ds)
    kv = (None, None, None, None)
    states = (None, None)
    nf_p, nf_s = [], []
    for i in range(depth):
        j = i // 2
        if i % 2 == 0:
            h_p, h_s, kv, lf_p, lf_s = _fox_layer(
                h_p, h_s, dims, norm_a, w_in_a_t, b_forget, w_out_a, j, a_heads, kv,
                (cache_k, cache_v, cache_lf_t, page_table))
            nf_p.append(lf_p)
            nf_s.append(lf_s)
        else:
            h_p, h_s, states = _gla_layer(
                h_p, h_s, dims, norm_b, w_in_b_t, w_gate_up, b_gate, onorm_b, w_out_b, j, b_heads,
                state_gla, states)
    y_prompt = _rmsnorm(h_p, final_norm, F32).reshape(bsz, t, d)[:, N_META:]
    y_sample = _rmsnorm(h_s, final_norm, F32).reshape(db, ds, d)
    n_a = norm_a.shape[0]
    kp_shape = (n_a, bsz, t, a_heads, dh)
    ks_shape = (n_a, db, ds, a_heads, dh)
    return (y_prompt, y_sample, kv[0].reshape(kp_shape), kv[1].reshape(kp_shape), jnp.stack(nf_p),
            kv[2].reshape(ks_shape), kv[3].reshape(ks_shape), jnp.stack(nf_s), states[0], states[1])
```

```python
import functools

import jax
import jax.numpy as jnp
from jax import lax
from jax.experimental import pallas as pl
from jax.experimental.pallas import tpu as pltpu

EPS = 1e-6
N_META = 16
GLA_GATE_TAU = 16.0
LOG2E = 1.4426950408889634
LANES = 128
SUBLANES = 8
VMEM_LIMIT_BYTES = 56 * 1024 * 1024
F32 = jnp.float32
BF16 = jnp.bfloat16
HIGHEST = lax.Precision.HIGHEST


def _cparams(*sem):
    return pltpu.CompilerParams(dimension_semantics=sem, vmem_limit_bytes=VMEM_LIMIT_BYTES)


def _pick_tile(n, target, mult):
    best = None
    for t in range(mult, min(n, target) + 1, mult):
        if n % t == 0:
            best = t
    return best if best is not None else n


def _dot(a, b):
    return jnp.dot(a, b, preferred_element_type=F32)


def _dot_nt(a, b, precision=None):
    return lax.dot_general(a, b, (((1,), (1,)), ((), ())), precision=precision,
                           preferred_element_type=F32)


def _dot_tn(a, b):
    return lax.dot_general(a, b, (((0,), (0,)), ((), ())), preferred_element_type=F32)


def _dot_exact(a, b):
    return jnp.dot(a, b, precision=HIGHEST, preferred_element_type=F32)


def _log_sigmoid(x):
    return jnp.minimum(x, 0.0) - jnp.log1p(jnp.exp(-jnp.abs(x)))


def _silu(x):
    return x * (1.0 / (1.0 + jnp.exp(-x)))


def _iota2(shape, dim):
    return lax.broadcasted_iota(jnp.int32, shape, dim)


def _rmsnorm_kernel(x_ref, g_ref, o_ref):
    x = x_ref[...]
    ms = jnp.mean(x * x, axis=-1, keepdims=True)
    o_ref[...] = (x * lax.rsqrt(ms + EPS) * g_ref[...]).astype(o_ref.dtype)


def _rmsnorm(x, g, out_dtype):
    m, d = x.shape
    tm = _pick_tile(m, 512, 16)
    return pl.pallas_call(
        _rmsnorm_kernel,
        grid=(m // tm,),
        in_specs=[pl.BlockSpec((tm, d), lambda i: (i, 0)),
                  pl.BlockSpec((1, d), lambda i: (0, 0))],
        out_specs=pl.BlockSpec((tm, d), lambda i: (i, 0)),
        out_shape=jax.ShapeDtypeStruct((m, d), out_dtype),
        compiler_params=_cparams("arbitrary"),
        name="rmsnorm",
    )(x, g.reshape(1, d))


def _proj_kernel(*refs, w_is_nk, has_res, n_prev, out_scale):
    x_ref, xs_ref, w_ref = refs[:3]
    r_ref, rs_ref = refs[3:5] if has_res else (None, None)
    o_ref, os_ref, wb_ref = refs[(5 if has_res else 3) + n_prev:]

    def mm(x):
        return _dot_nt(x, wb_ref[...]) if w_is_nk else _dot(x, wb_ref[...])

    @pl.when(pl.program_id(1) == 0)
    def _():
        wb_ref[...] = w_ref[...].astype(BF16)
        acc_s = mm(xs_ref[...])
        if has_res:
            acc_s = rs_ref[...] + acc_s
        os_ref[...] = acc_s.astype(os_ref.dtype)

    acc = mm(x_ref[...])
    if has_res:
        acc = r_ref[...] + acc
    if out_scale is not None:
        acc = acc * out_scale
    o_ref[...] = acc.astype(o_ref.dtype)


def _proj(x, xs, w, layer, col0, ncols, out_dtype, *, w_is_nk, residual=None, stack=None,
          out_scale=None, tn=512, tm_target=704):
    m, k = x.shape
    ms = xs.shape[0]
    tm = _pick_tile(m, tm_target, 16)
    assert ncols % tn == 0 and col0 % tn == 0
    cb = col0 // tn
    if w_is_nk:
        w_spec = pl.BlockSpec((None, tn, k), lambda n, i: (layer, cb + n, 0))
        wb_shape = (tn, k)
    else:
        w_spec = pl.BlockSpec((None, k, tn), lambda n, i: (layer, 0, cb + n))
        wb_shape = (k, tn)
    in_specs = [pl.BlockSpec((tm, k), lambda n, i: (i, 0)),
                pl.BlockSpec((ms, k), lambda n, i: (0, 0)), w_spec]
    args = [x, xs, w]
    aliases = {}
    if residual is not None:
        in_specs += [pl.BlockSpec((tm, tn), lambda n, i: (i, n)),
                     pl.BlockSpec((ms, tn), lambda n, i: (0, n))]
        args += list(residual)
        aliases = {3: 0, 4: 1}
    n_prev = 0
    if stack is None:
        out_specs = [pl.BlockSpec((tm, tn), lambda n, i: (i, n)),
                     pl.BlockSpec((ms, tn), lambda n, i: (0, n))]
        out_shape = [jax.ShapeDtypeStruct((m, ncols), out_dtype),
                     jax.ShapeDtypeStruct((ms, ncols), F32)]
    else:
        n_layers, prev, prev_s = stack
        out_specs = [pl.BlockSpec((None, tm, tn), lambda n, i: (layer, i, n)),
                     pl.BlockSpec((None, ms, tn), lambda n, i: (layer, 0, n))]
        out_shape = [jax.ShapeDtypeStruct((n_layers, m, ncols), out_dtype),
                     jax.ShapeDtypeStruct((n_layers, ms, ncols), F32)]
        if prev is not None:
            n_prev = 2
            in_specs += [pl.BlockSpec(memory_space=pl.ANY)] * 2
            aliases = {len(args): 0, len(args) + 1: 1}
            args += [prev, prev_s]
    return pl.pallas_call(
        functools.partial(_proj_kernel, w_is_nk=w_is_nk, has_res=residual is not None,
                          n_prev=n_prev, out_scale=out_scale),
        grid=(ncols // tn, m // tm),
        in_specs=in_specs,
        out_specs=out_specs,
        out_shape=out_shape,
        scratch_shapes=[pltpu.VMEM(wb_shape, BF16)],
        input_output_aliases=aliases,
        compiler_params=_cparams("arbitrary", "arbitrary"),
        name="proj",
    )(*args)


def _tail_kernel(x_ref, w_ref, o_ref):
    o_ref[...] = _dot_nt(x_ref[...], w_ref[...].astype(BF16))


def _tail_proj(x, w_tail):
    m, k = x.shape
    tm = _pick_tile(m, 704, 16)
    return pl.pallas_call(
        _tail_kernel,
        grid=(m // tm,),
        in_specs=[pl.BlockSpec((tm, k), lambda i: (i, 0)),
                  pl.BlockSpec((LANES, k), lambda i: (0, 0))],
        out_specs=pl.BlockSpec((tm, LANES), lambda i: (i, 0)),
        out_shape=jax.ShapeDtypeStruct((m, LANES), F32),
        compiler_params=_cparams("arbitrary"),
        name="tail_proj",
    )(x, w_tail)


def _logf_kernel(z_ref, b_ref, lf_ref, c_ref, carry_ref, *, seg):
    tr = z_ref.shape[0]
    lf = _log_sigmoid(z_ref[...] + b_ref[...])
    lf_ref[...] = lf
    row = _iota2((tr, tr), 0)
    col = _iota2((tr, tr), 1)
    if seg >= tr:
        @pl.when(pl.program_id(1) == 0)
        def _():
            carry_ref[...] = jnp.zeros_like(carry_ref)

        tri = (col <= row).astype(F32)
        c = _dot_exact(tri, lf) + carry_ref[...]
        c_ref[...] = c
        carry_ref[...] = c[tr - 1:tr, :]
    else:
        tri = ((col <= row) & ((row // seg) == (col // seg))).astype(F32)
        c_ref[...] = _dot_exact(tri, lf)


def _logf_cumsum(z, bias, nb, t):
    m = nb * t
    if t >= 128:
        tr = _pick_tile(t, 704, 8)
        grid = (nb, t // tr)
        spec = pl.BlockSpec((tr, LANES), lambda b, i: (b * (t // tr) + i, 0))
    else:
        tr = m
        grid = (1, 1)
        spec = pl.BlockSpec((tr, LANES), lambda b, i: (0, 0))
    return pl.pallas_call(
        functools.partial(_logf_kernel, seg=t),
        grid=grid,
        in_specs=[spec, pl.BlockSpec((1, LANES), lambda b, i: (0, 0))],
        out_specs=[spec, spec],
        out_shape=[jax.ShapeDtypeStruct((m, LANES), F32)] * 2,
        scratch_shapes=[pltpu.VMEM((1, LANES), F32)],
        compiler_params=_cparams("arbitrary", "arbitrary"),
        name="logf_cumsum",
    )(z, bias)


def _fox_prompt_kernel(q_ref, k_ref, v_ref, g_ref, cm_ref, cs_ref, y_ref, kb_ref, vb_ref,
                       *, n_meta, tq):
    t_len = q_ref.shape[0]
    nblk = (t_len - n_meta) // tq
    kb_ref[...] = k_ref[...].astype(BF16)
    vb_ref[...] = v_ref[...].astype(BF16)
    km = kb_ref[0:n_meta, :]
    vm = vb_ref[0:n_meta, :]
    cm = cm_ref[...] * LOG2E
    cs = cs_ref[...] * LOG2E

    def emit(r0, n, acc, l):
        o = acc * (1.0 / l)
        y_ref[r0:r0 + n, :] = (o * _silu(g_ref[r0:r0 + n, :])).astype(y_ref.dtype)

    s = _dot_nt(q_ref[0:n_meta, :], km) - cm
    s = jnp.where(_iota2(s.shape, 1) <= _iota2(s.shape, 0), s, -jnp.inf)
    m = jnp.max(s, axis=1, keepdims=True)
    p = jnp.exp2(s - m)
    emit(0, n_meta, _dot(p.astype(BF16), vm), jnp.sum(p, axis=1, keepdims=True))

    for i in range(nblk):
        r0 = n_meta + i * tq
        q = q_ref[r0:r0 + tq, :]
        s0 = _dot_nt(q, km) - cm
        m_i = jnp.max(s0, axis=1, keepdims=True)
        p0 = jnp.exp2(s0 - m_i)
        l_i = jnp.broadcast_to(jnp.sum(p0, axis=1, keepdims=True) * (1.0 / LANES), (tq, LANES))
        acc = _dot(p0.astype(BF16), vm)
        for j in range(i + 1):
            k0 = n_meta + j * tq
            s_ = _dot_nt(q, kb_ref[k0:k0 + tq, :]) - cs[:, j * tq:(j + 1) * tq]
            if j == i:
                s_ = jnp.where(_iota2(s_.shape, 1) <= _iota2(s_.shape, 0), s_, -jnp.inf)
            m_n = jnp.maximum(m_i, jnp.max(s_, axis=1, keepdims=True))
            alpha = jnp.exp2(m_i - m_n)
            p_ = jnp.exp2(s_ - m_n)
            l_i = alpha * l_i + sum(p_[:, c0:c0 + LANES] for c0 in range(0, tq, LANES))
            acc = alpha * acc + _dot(p_.astype(BF16), vb_ref[k0:k0 + tq, :])
            m_i = m_n
        emit(r0, tq, acc, jnp.sum(l_i, axis=1, keepdims=True))


def _fox_prompt(q, k_all, v_all, g, c_meta, c_seq, layer, n_heads):
    b, t, w = q.shape
    dh = w // n_heads
    s_len = t - N_META
    tq = _pick_tile(s_len, 256, LANES)
    blk = lambda: pl.BlockSpec((None, t, dh), lambda bi, hi: (bi, 0, hi))
    lblk = lambda: pl.BlockSpec((None, None, t, dh), lambda bi, hi: (layer, bi, 0, hi))
    return pl.pallas_call(
        functools.partial(_fox_prompt_kernel, n_meta=N_META, tq=tq),
        grid=(b, n_heads),
        in_specs=[blk(), lblk(), lblk(), blk(),
                  pl.BlockSpec((None, None, 1, N_META), lambda bi, hi: (bi, hi, 0, 0)),
                  pl.BlockSpec((None, None, 1, s_len), lambda bi, hi: (bi, hi, 0, 0))],
        out_specs=blk(),
        out_shape=jax.ShapeDtypeStruct((b, t, w), BF16),
        scratch_shapes=[pltpu.VMEM((t, dh), BF16), pltpu.VMEM((t, dh), BF16)],
        compiler_params=_cparams("arbitrary", "arbitrary"),
        name="fox_prompt_attn",
    )(q, k_all, v_all, g, c_meta, c_seq)


SAMPLE_PAGES_PER_STEP = (4, 2, 1)


def _fox_sample_kernel(pt_ref, q_ref, g_ref, kn_ref, vn_ref, cn_ref, *rest, scale, pps):
    del pt_ref
    kp_refs, vp_refs, lfp_refs = rest[:pps], rest[pps:2 * pps], rest[2 * pps:3 * pps]
    y_ref, m_ref, l_ref, acc_ref, carry_ref, wq_ref, lr_ref, neg_ref = rest[3 * pps:]
    p_idx = pl.program_id(1)
    n_steps = pl.num_programs(1)
    page, n_heads, dh = kp_refs[0].shape
    n_new = kn_ref.shape[0]
    n_groups = n_heads // SUBLANES
    grp_rows = SUBLANES * n_new
    grp_keys = SUBLANES * page

    @pl.when(p_idx == 0)
    def _():
        m_ref[...] = jnp.full_like(m_ref, -jnp.inf)
        l_ref[...] = jnp.zeros_like(l_ref)
        acc_ref[...] = jnp.zeros_like(acc_ref)
        carry_ref[...] = jnp.zeros_like(carry_ref)
        wq_ref[...] = (q_ref[...] * (scale * LOG2E)).astype(BF16)
        lr_ref[...] = (_iota2(lr_ref.shape, 0) > _iota2(lr_ref.shape, 1) // SUBLANES).astype(F32)
        same_head = (_iota2(neg_ref.shape, 1) % SUBLANES) == (_iota2(neg_ref.shape, 0) // n_new)
        neg_ref[...] = jnp.where(same_head, 0.0, -jnp.inf)

    def update(state, s, vb):
        m_i, l_i, acc = state
        m_n = jnp.maximum(m_i, jnp.max(s, axis=1, keepdims=True))
        alpha = jnp.exp2(m_i - m_n)
        p = jnp.exp2(s - m_n)
        return (m_n, alpha * l_i + jnp.sum(p, axis=1, keepdims=True),
                alpha * acc + _dot(p.astype(BF16), vb))

    def load_state():
        m_all, l_all, acc_all = m_ref[...], l_ref[...], acc_ref[...]
        rows = [slice(g * grp_rows, (g + 1) * grp_rows) for g in range(n_groups)]
        return [(m_all[r, :], l_all[r, :], acc_all[r, :]) for r in rows]

    def store_state(states):
        for ref, parts in zip((m_ref, l_ref, acc_ref), zip(*states)):
            ref[...] = parts[0] if n_groups == 1 else jnp.concatenate(parts, axis=0)

    def group(ref, g):
        blk = ref[:, g * SUBLANES:(g + 1) * SUBLANES, :]
        return blk.reshape(blk.shape[0] * SUBLANES, dh).astype(BF16)

    carry = carry_ref[...]
    suffs = []
    for i in range(pps):
        lf = lfp_refs[i][...]
        suffs.append((_dot_exact(lf, lr_ref[...]) + carry) * LOG2E)
        carry = carry + jnp.sum(lf, axis=1, keepdims=True)
    carry_ref[...] = carry
    own_head = (_iota2((SUBLANES, grp_keys), 1) % SUBLANES) == _iota2((SUBLANES, grp_keys), 0)
    states = load_state()
    for g in range(n_groups):
        wq = wq_ref[g * grp_rows:(g + 1) * grp_rows, :]
        s_parts = []
        for i in range(pps):
            sg = suffs[i][g * SUBLANES:(g + 1) * SUBLANES, :]
            bias = jnp.sum(jnp.where(own_head, sg, 0.0), axis=0, keepdims=True)
            s_parts.append(_dot_nt(wq, group(kp_refs[i], g)) + (bias + neg_ref[...]))
        s_past = s_parts[0] if pps == 1 else jnp.concatenate(s_parts, axis=1)
        v_parts = [group(vp_refs[i], g) for i in range(pps)]
        states[g] = update(states[g], s_past, v_parts[0] if pps == 1 else jnp.concatenate(v_parts, axis=0))
    store_state(states)

    @pl.when(p_idx == n_steps - 1)
    def _():
        new_keys = SUBLANES * n_new
        r_ = _iota2((grp_rows, new_keys), 0)
        c_ = _iota2((grp_rows, new_keys), 1)
        valid = ((c_ % SUBLANES) == (r_ // n_new)) & ((c_ // SUBLANES) <= (r_ % n_new))
        final = load_state()
        for g in range(n_groups):
            wq = wq_ref[g * grp_rows:(g + 1) * grp_rows, :]
            s_new = _dot_nt(wq, group(kn_ref, g)) + cn_ref[g] * LOG2E
            final[g] = update(final[g], jnp.where(valid, s_new, -jnp.inf), group(vn_ref, g))
        o = jnp.concatenate([acc * (1.0 / l_f) for _, l_f, acc in final], axis=0)
        y_ref[...] = (o * _silu(g_ref[...])).astype(y_ref.dtype)


def _fox_sample(q, g_new, k_all, v_all, c_new, cache_k, cache_v, cache_lf_t, layer, page_table, n_heads):
    db, s_new, w = q.shape
    n_pages = page_table.shape[1]
    ps = cache_k.shape[2]
    dh = w // n_heads
    n_layers = k_all.shape[0]
    n_groups = n_heads // SUBLANES
    rows = s_new * n_heads
    to_rows = lambda a: a.reshape(db, s_new, n_heads, dh).transpose(0, 2, 1, 3).reshape(db, rows, dh)
    q_r, g_r = to_rows(q), to_rows(g_new)
    kn = k_all.reshape(n_layers, db, s_new, n_heads, dh)
    vn = v_all.reshape(n_layers, db, s_new, n_heads, dh)
    cn = -c_new.reshape(db, s_new, n_groups, SUBLANES).transpose(0, 2, 1, 3)
    cn = cn.reshape(db, n_groups, 1, s_new * SUBLANES)
    per_b = lambda: pl.BlockSpec((None, rows, dh), lambda b, p, pt: (b, 0, 0))
    new_spec = lambda: pl.BlockSpec((None, None, s_new, n_heads, dh), lambda b, p, pt: (layer, b, 0, 0, 0))
    pps = max(c for c in SAMPLE_PAGES_PER_STEP if n_pages % c == 0)

    def page_spec(i):
        return pl.BlockSpec(
            (None, None, ps, n_heads, dh),
            lambda b, p, pt: (layer, pt[b, n_pages - 1 - (p * pps + i)], 0, 0, 0))

    def lf_spec(i):
        return pl.BlockSpec(
            (None, None, n_heads, ps),
            lambda b, p, pt: (layer, pt[b, n_pages - 1 - (p * pps + i)], 0, 0))

    grid_spec = pltpu.PrefetchScalarGridSpec(
        num_scalar_prefetch=1,
        grid=(db, n_pages // pps),
        in_specs=[per_b(), per_b(), new_spec(), new_spec(),
                  pl.BlockSpec((None, n_groups, 1, s_new * SUBLANES), lambda b, p, pt: (b, 0, 0, 0))]
                 + [page_spec(i) for i in range(pps)] + [page_spec(i) for i in range(pps)]
                 + [lf_spec(i) for i in range(pps)],
        out_specs=per_b(),
        scratch_shapes=[pltpu.VMEM((rows, 1), F32), pltpu.VMEM((rows, 1), F32),
                        pltpu.VMEM((rows, dh), F32), pltpu.VMEM((n_heads, 1), F32),
                        pltpu.VMEM((rows, dh), BF16), pltpu.VMEM((ps, ps * SUBLANES), F32),
                        pltpu.VMEM((SUBLANES * s_new, ps * SUBLANES), F32)],
    )
    y = pl.pallas_call(
        functools.partial(_fox_sample_kernel, scale=dh ** -0.5, pps=pps),
        grid_spec=grid_spec,
        out_shape=jax.ShapeDtypeStruct((db, rows, dh), BF16),
        compiler_params=_cparams("arbitrary", "arbitrary"),
        name="fox_sample_attn",
    )(page_table, q_r, g_r, kn, vn, cn, *([cache_k] * pps), *([cache_v] * pps), *([cache_lf_t] * pps))
    return y.reshape(db, n_heads, s_new, dh).transpose(0, 2, 1, 3).reshape(db, s_new, w)


GLA_CHUNK = 128


def _gla_chunk(r0, c, refs, wa, q_scale):
    q_ref, k_ref, v_ref, g_ref, a_ref, bg_ref, on_ref, y_ref, s_ref = refs
    rows = pl.ds(r0, c)
    q = q_ref[rows, :] * q_scale
    k = k_ref[rows, :]
    v = v_ref[rows, :].astype(BF16)
    dk = q.shape[1]
    x = _dot(a_ref[rows, :].astype(BF16), wa) + bg_ref[...]
    la = _log_sigmoid(x) * (1.0 / GLA_GATE_TAU)
    row = _iota2((c, c), 0)
    col = _iota2((c, c), 1)
    b = _dot_exact((col <= row).astype(F32), la)
    b_last = b[c - 1:c, :]

    s_old = s_ref[...]
    o = _dot((q * jnp.exp(b)).astype(BF16), s_old.astype(BF16))

    a_mat = jnp.where(row == col, jnp.sum(q * k, axis=1, keepdims=True), 0.0)
    xor = row ^ col
    sub = _iota2((c, dk), 0) % SUBLANES

    def tile_rows(first, stride, n):
        pieces = [jnp.broadcast_to(b[r:r + 1, :], (n, dk)) for r in range(first, c, stride)]
        return pieces[0] if len(pieces) == 1 else jnp.concatenate(pieces, axis=0)

    blk = 2
    while blk <= c:
        half = blk // 2
        if blk < SUBLANES:
            b_mid = tile_rows(half - 1, SUBLANES, SUBLANES)
            for s0 in range(blk, SUBLANES, blk):
                b_mid = jnp.where(sub >= s0, tile_rows(s0 + half - 1, SUBLANES, SUBLANES), b_mid)
        else:
            b_mid = tile_rows(half - 1, blk, blk)
        q_t = (q * jnp.exp(jnp.minimum(b - b_mid, 0.0))).astype(BF16)
        k_t = (k * jnp.exp(jnp.minimum(b_mid - b, 0.0))).astype(BF16)
        a_mat = jnp.where(xor >= half, _dot_nt(q_t, k_t), a_mat)
        blk *= 2
    a_mat = jnp.where(col <= row, a_mat, 0.0)
    o = o + _dot(a_mat.astype(BF16), v)

    k_end = (k * jnp.exp(b_last - b)).astype(BF16)
    decay = jnp.transpose(jnp.broadcast_to(jnp.exp(b_last), (8, dk)))[:, 0:1]
    s_ref[...] = decay * s_old + _dot_tn(k_end, v)

    o = o * lax.rsqrt(jnp.mean(o * o, axis=-1, keepdims=True) + EPS) * on_ref[...]
    y_ref[rows, :] = (o * _silu(g_ref[rows, :])).astype(y_ref.dtype)


def _gla_kernel(*refs, head, n_chunks, has_state, has_prev, q_scale):
    q_ref, k_ref, v_ref, g_ref, a_ref, wa_ref, bg_ref, on_ref = refs[:8]
    y_ref, so_ref, s_ref = refs[8 + has_state + has_prev:]
    if has_state:
        s_ref[...] = refs[8][...]
    else:
        s_ref[...] = jnp.zeros_like(s_ref)
    wa = wa_ref[...].astype(BF16)
    crefs = (q_ref, k_ref, v_ref, g_ref, a_ref, bg_ref, on_ref, y_ref, s_ref)
    if head:
        _gla_chunk(0, head, crefs, wa, q_scale)

    def body(i, carry):
        _gla_chunk(pl.multiple_of(head + i * GLA_CHUNK, 16), GLA_CHUNK, crefs, wa, q_scale)
        return carry

    if n_chunks:
        lax.fori_loop(0, n_chunks, body, 0)
    so_ref[...] = s_ref[...]


def _gla(zq, zk, zv, zg, za, wa2, b_gate, onorm, state0, layer, n_layers, s_prev, n_heads):
    b, t, kw = zq.shape
    vw = zv.shape[2]
    dk, dv = kw // n_heads, vw // n_heads
    head = t % GLA_CHUNK
    n_chunks = t // GLA_CHUNK
    assert head in (0, 8, 16, 32, 64)
    tok = lambda last: pl.BlockSpec((None, t, last), lambda bi, hi: (bi, 0, hi))
    in_specs = [tok(dk), tok(dk), tok(dv), tok(dv),
                pl.BlockSpec((None, t, LANES), lambda bi, hi: (bi, 0, 0)),
                pl.BlockSpec((LANES, dk), lambda bi, hi: (0, hi)),
                pl.BlockSpec((1, dk), lambda bi, hi: (0, hi)),
                pl.BlockSpec((1, dv), lambda bi, hi: (0, 0))]
    args = [zq, zk, zv, zg, za, wa2, b_gate, onorm]
    if state0 is not None:
        in_specs.append(pl.BlockSpec((None, None, None, dk, dv), lambda bi, hi: (layer, bi, hi, 0, 0)))
        args.append(state0)
    aliases = {}
    if s_prev is not None:
        in_specs.append(pl.BlockSpec(memory_space=pl.ANY))
        aliases = {len(args): 1}
        args.append(s_prev)
    return pl.pallas_call(
        functools.partial(_gla_kernel, head=head, n_chunks=n_chunks, has_state=state0 is not None,
                          has_prev=s_prev is not None, q_scale=dk ** -0.5),
        grid=(b, n_heads),
        in_specs=in_specs,
        out_specs=[tok(dv),
                   pl.BlockSpec((None, None, None, dk, dv), lambda bi, hi: (layer, bi, hi, 0, 0))],
        out_shape=[jax.ShapeDtypeStruct((b, t, vw), BF16),
                   jax.ShapeDtypeStruct((n_layers, b, n_heads, dk, dv), F32)],
        scratch_shapes=[pltpu.VMEM((dk, dv), F32)],
        input_output_aliases=aliases,
        compiler_params=_cparams("arbitrary", "arbitrary"),
        name="gla",
    )(*args)


def _pad_rows(x):
    return jnp.pad(x, ((0, LANES - x.shape[0]), (0, 0)))


def _fox_layer(h_p, h_s, dims, norm, w_in_t, b_f, w_out, layer, n_heads, prev, sample_ctx):
    bsz, t, db, ds = dims
    d = h_p.shape[1]
    aw = w_out.shape[1]
    n_layers = w_out.shape[0]
    dh = aw // n_heads
    kp_prev, vp_prev, ks_prev, vs_prev = prev
    u_p = _rmsnorm(h_p, norm[layer], BF16)
    u_s = _rmsnorm(h_s, norm[layer], BF16)
    proj_in = functools.partial(_proj, u_p, u_s, w_in_t, layer, w_is_nk=True)
    q_p, q_s = proj_in(0, aw, BF16, out_scale=dh ** -0.5 * LOG2E)
    k_p, k_s = proj_in(aw, aw, F32, stack=(n_layers, kp_prev, ks_prev))
    v_p, v_s = proj_in(2 * aw, aw, F32, stack=(n_layers, vp_prev, vs_prev))
    g_p, g_s = proj_in(3 * aw, aw, F32)
    w_tail = _pad_rows(w_in_t[layer, 4 * aw:, :])
    bias = jnp.pad(b_f[layer][None, :].astype(F32), ((0, 0), (0, LANES - n_heads)))
    lf_p, c_p = _logf_cumsum(_tail_proj(u_p, w_tail), bias, bsz, t)
    lf_s, c_s = _logf_cumsum(_tail_proj(u_s, w_tail), bias, db, ds)
    lf_p = lf_p[:, :n_heads].reshape(bsz, t, n_heads)
    lf_s = lf_s[:, :n_heads].reshape(db, ds, n_heads)
    c_t = jnp.transpose(c_p[:, :n_heads].reshape(bsz, t, n_heads), (0, 2, 1))[:, :, None, :]
    y_p = _fox_prompt(q_p.reshape(bsz, t, aw), k_p.reshape(n_layers, bsz, t, aw),
                      v_p.reshape(n_layers, bsz, t, aw), g_p.reshape(bsz, t, aw),
                      c_t[..., :N_META], c_t[..., N_META:], layer, n_heads)
    cache_k, cache_v, cache_lf_t, page_table = sample_ctx
    y_s = _fox_sample(q_s.reshape(db, ds, aw), g_s.reshape(db, ds, aw), k_s, v_s,
                      c_s[:, :n_heads].reshape(db, ds, n_heads),
                      cache_k, cache_v, cache_lf_t, layer, page_table, n_heads)
    h_p, h_s = _proj(y_p.reshape(bsz * t, aw), y_s.reshape(db * ds, aw), w_out, layer, 0, d, F32,
                     w_is_nk=False, residual=(h_p, h_s))
    return h_p, h_s, (k_p, v_p, k_s, v_s), lf_p, lf_s


def _gla_layer(h_p, h_s, dims, norm, w_in_t, w_a2, b_gate, onorm, w_out, layer, n_heads, state0, prev):
    bsz, t, db, ds = dims
    d = h_p.shape[1]
    vw = w_out.shape[1]
    kw = w_a2.shape[2]
    n_layers = w_out.shape[0]
    u_p = _rmsnorm(h_p, norm[layer], BF16)
    u_s = _rmsnorm(h_s, norm[layer], BF16)
    proj_in = functools.partial(_proj, u_p, u_s, w_in_t, layer, w_is_nk=True)
    zq_p, zq_s = proj_in(0, kw, F32)
    zk_p, zk_s = proj_in(kw, kw, F32)
    zv_p, zv_s = proj_in(2 * kw, vw, F32)
    zg_p, zg_s = proj_in(2 * kw + vw, vw, F32)
    w_tail = _pad_rows(w_in_t[layer, 2 * kw + 2 * vw:, :])
    za_p, za_s = _tail_proj(u_p, w_tail), _tail_proj(u_s, w_tail)
    wa2 = _pad_rows(w_a2[layer])
    bg, on = b_gate[layer][None, :], onorm[layer][None, :]
    y_p, st_p = _gla(zq_p.reshape(bsz, t, kw), zk_p.reshape(bsz, t, kw), zv_p.reshape(bsz, t, vw),
                     zg_p.reshape(bsz, t, vw), za_p.reshape(bsz, t, LANES), wa2, bg, on,
                     None, layer, n_layers, prev[0], n_heads)
    y_s, st_s = _gla(zq_s.reshape(db, ds, kw), zk_s.reshape(db, ds, kw), zv_s.reshape(db, ds, vw),
                     zg_s.reshape(db, ds, vw), za_s.reshape(db, ds, LANES), wa2, bg, on,
                     state0, layer, n_layers, prev[1], n_heads)
    h_p, h_s = _proj(y_p.reshape(bsz * t, vw), y_s.reshape(db * ds, vw), w_out, layer, 0, d, F32,
                     w_is_nk=False, residual=(h_p, h_s))
    return h_p, h_s, (st_p, st_s)


def kernel(x_prompt, x_sample, cache_k, cache_v, cache_logf, state_gla, page_table, meta_tokens,
           norm_a, w_in_a, b_forget, w_out_a, norm_b, w_in_b, w_gate_up, b_gate, onorm_b, w_out_b,
           final_norm):
    bsz, seq, d = x_prompt.shape
    db, ds, _ = x_sample.shape
    a_heads = b_forget.shape[1]
    b_heads = state_gla.shape[2]
    dh = w_out_a.shape[1] // a_heads
    depth = norm_a.shape[0] + norm_b.shape[0]
    t = N_META + seq
    meta = jnp.broadcast_to(meta_tokens.astype(x_prompt.dtype)[None], (bsz, N_META, d))
    h_p = jnp.concatenate([meta, x_prompt], axis=1).reshape(bsz * t, d)
    h_s = x_sample.reshape(db * ds, d)
    w_in_a_t = jnp.swapaxes(w_in_a, 1, 2)
    w_in_b_t = jnp.swapaxes(w_in_b, 1, 2)
    cache_lf_t = jnp.swapaxes(cache_logf, 2, 3)
    dims = (bsz, t, db, ds)
    kv = (None, None, None, None)
    states = (None, None)
    nf_p, nf_s = [], []
    for i in range(depth):
        j = i // 2
        if i % 2 == 0:
            h_p, h_s, kv, lf_p, lf_s = _fox_layer(
                h_p, h_s, dims, norm_a, w_in_a_t, b_forget, w_out_a, j, a_heads, kv,
                (cache_k, cache_v, cache_lf_t, page_table))
            nf_p.append(lf_p)
            nf_s.append(lf_s)
        else:
            h_p, h_s, states = _gla_layer(
                h_p, h_s, dims, norm_b, w_in_b_t, w_gate_up, b_gate, onorm_b, w_out_b, j, b_heads,
                state_gla, states)
    y_prompt = _rmsnorm(h_p, final_norm, F32).reshape(bsz, t, d)[:, N_META:]
    y_sample = _rmsnorm(h_s, final_norm, F32).reshape(db, ds, d)
    n_a = norm_a.shape[0]
    kp_shape = (n_a, bsz, t, a_heads, dh)
    ks_shape = (n_a, db, ds, a_heads, dh)
    return (y_prompt, y_sample, kv[0].reshape(kp_shape), kv[1].reshape(kp_shape), jnp.stack(nf_p),
            kv[2].reshape(ks_shape), kv[3].reshape(ks_shape), jnp.stack(nf_s), states[0], states[1])
```

```python
import functools

import jax
import jax.numpy as jnp
from jax import lax
from jax.experimental import pallas as pl
from jax.experimental.pallas import tpu as pltpu

EPS = 1e-6
N_META = 16
GLA_GATE_TAU = 16.0
LOG2E = 1.4426950408889634
LANES = 128
SUBLANES = 8
VMEM_LIMIT_BYTES = 56 * 1024 * 1024
F32 = jnp.float32
BF16 = jnp.bfloat16
HIGHEST = lax.Precision.HIGHEST


def _cparams(*sem):
    return pltpu.CompilerParams(dimension_semantics=sem, vmem_limit_bytes=VMEM_LIMIT_BYTES)


def _pick_tile(n, target, mult):
    best = None
    for t in range(mult, min(n, target) + 1, mult):
        if n % t == 0:
            best = t
    return best if best is not None else n


def _dot(a, b):
    return jnp.dot(a, b, preferred_element_type=F32)


def _dot_nt(a, b, precision=None):
    return lax.dot_general(a, b, (((1,), (1,)), ((), ())), precision=precision,
                           preferred_element_type=F32)


def _dot_tn(a, b):
    return lax.dot_general(a, b, (((0,), (0,)), ((), ())), preferred_element_type=F32)


def _dot_exact(a, b):
    return jnp.dot(a, b, precision=HIGHEST, preferred_element_type=F32)


def _log_sigmoid(x):
    return jnp.minimum(x, 0.0) - jnp.log1p(jnp.exp(-jnp.abs(x)))


def _silu(x):
    return x * (1.0 / (1.0 + jnp.exp(-x)))


def _iota2(shape, dim):
    return lax.broadcasted_iota(jnp.int32, shape, dim)


def _rmsnorm_kernel(x_ref, g_ref, o_ref):
    x = x_ref[...]
    ms = jnp.mean(x * x, axis=-1, keepdims=True)
    o_ref[...] = (x * lax.rsqrt(ms + EPS) * g_ref[...]).astype(o_ref.dtype)


def _rmsnorm(x, g, out_dtype):
    m, d = x.shape
    tm = _pick_tile(m, 512, 16)
    return pl.pallas_call(
        _rmsnorm_kernel,
        grid=(m // tm,),
        in_specs=[pl.BlockSpec((tm, d), lambda i: (i, 0)),
                  pl.BlockSpec((1, d), lambda i: (0, 0))],
        out_specs=pl.BlockSpec((tm, d), lambda i: (i, 0)),
        out_shape=jax.ShapeDtypeStruct((m, d), out_dtype),
        compiler_params=_cparams("arbitrary"),
        name="rmsnorm",
    )(x, g.reshape(1, d))


def _proj_kernel(*refs, w_is_nk, has_res, n_prev, out_scale):
    x_ref, xs_ref, w_ref = refs[:3]
    r_ref, rs_ref = refs[3:5] if has_res else (None, None)
    o_ref, os_ref, wb_ref = refs[(5 if has_res else 3) + n_prev:]

    def mm(x):
        return _dot_nt(x, wb_ref[...]) if w_is_nk else _dot(x, wb_ref[...])

    @pl.when(pl.program_id(1) == 0)
    def _():
        wb_ref[...] = w_ref[...].astype(BF16)
        acc_s = mm(xs_ref[...])
        if has_res:
            acc_s = rs_ref[...] + acc_s
        os_ref[...] = acc_s.astype(os_ref.dtype)

    acc = mm(x_ref[...])
    if has_res:
        acc = r_ref[...] + acc
    if out_scale is not None:
        acc = acc * out_scale
    o_ref[...] = acc.astype(o_ref.dtype)


def _proj(x, xs, w, layer, col0, ncols, out_dtype, *, w_is_nk, residual=None, stack=None,
          out_scale=None, tn=512, tm_target=704):
    m, k = x.shape
    ms = xs.shape[0]
    tm = _pick_tile(m, tm_target, 16)
    assert ncols % tn == 0 and col0 % tn == 0
    cb = col0 // tn
    if w_is_nk:
        w_spec = pl.BlockSpec((None, tn, k), lambda n, i: (layer, cb + n, 0))
        wb_shape = (tn, k)
    else:
        w_spec = pl.BlockSpec((None, k, tn), lambda n, i: (layer, 0, cb + n))
        wb_shape = (k, tn)
    in_specs = [pl.BlockSpec((tm, k), lambda n, i: (i, 0)),
                pl.BlockSpec((ms, k), lambda n, i: (0, 0)), w_spec]
    args = [x, xs, w]
    aliases = {}
    if residual is not None:
        in_specs += [pl.BlockSpec((tm, tn), lambda n, i: (i, n)),
                     pl.BlockSpec((ms, tn), lambda n, i: (0, n))]
        args += list(residual)
        aliases = {3: 0, 4: 1}
    n_prev = 0
    if stack is None:
        out_specs = [pl.BlockSpec((tm, tn), lambda n, i: (i, n)),
                     pl.BlockSpec((ms, tn), lambda n, i: (0, n))]
        out_shape = [jax.ShapeDtypeStruct((m, ncols), out_dtype),
                     jax.ShapeDtypeStruct((ms, ncols), F32)]
    else:
        n_layers, prev, prev_s = stack
        out_specs = [pl.BlockSpec((None, tm, tn), lambda n, i: (layer, i, n)),
                     pl.BlockSpec((None, ms, tn), lambda n, i: (layer, 0, n))]
        out_shape = [jax.ShapeDtypeStruct((n_layers, m, ncols), out_dtype),
                     jax.ShapeDtypeStruct((n_layers, ms, ncols), F32)]
        if prev is not None:
            n_prev = 2
            in_specs += [pl.BlockSpec(memory_space=pl.ANY)] * 2
            aliases = {len(args): 0, len(args) + 1: 1}
            args += [prev, prev_s]
    return pl.pallas_call(
        functools.partial(_proj_kernel, w_is_nk=w_is_nk, has_res=residual is not None,
                          n_prev=n_prev, out_scale=out_scale),
        grid=(ncols // tn, m // tm),
        in_specs=in_specs,
        out_specs=out_specs,
        out_shape=out_shape,
        scratch_shapes=[pltpu.VMEM(wb_shape, BF16)],
        input_output_aliases=aliases,
        compiler_params=_cparams("arbitrary", "arbitrary"),
        name="proj",
    )(*args)


def _tail_kernel(x_ref, w_ref, o_ref):
    o_ref[...] = _dot_nt(x_ref[...], w_ref[...].astype(BF16))


def _tail_proj(x, w_tail):
    m, k = x.shape
    tm = _pick_tile(m, 704, 16)
    return pl.pallas_call(
        _tail_kernel,
        grid=(m // tm,),
        in_specs=[pl.BlockSpec((tm, k), lambda i: (i, 0)),
                  pl.BlockSpec((LANES, k), lambda i: (0, 0))],
        out_specs=pl.BlockSpec((tm, LANES), lambda i: (i, 0)),
        out_shape=jax.ShapeDtypeStruct((m, LANES), F32),
        compiler_params=_cparams("arbitrary"),
        name="tail_proj",
    )(x, w_tail)


def _logf_kernel(z_ref, b_ref, lf_ref, c_ref, carry_ref, *, seg):
    tr = z_ref.shape[0]
    lf = _log_sigmoid(z_ref[...] + b_ref[...])
    lf_ref[...] = lf
    row = _iota2((tr, tr), 0)
    col = _iota2((tr, tr), 1)
    if seg >= tr:
        @pl.when(pl.program_id(1) == 0)
        def _():
            carry_ref[...] = jnp.zeros_like(carry_ref)

        tri = (col <= row).astype(F32)
        c = _dot_exact(tri, lf) + carry_ref[...]
        c_ref[...] = c
        carry_ref[...] = c[tr - 1:tr, :]
    else:
        tri = ((col <= row) & ((row // seg) == (col // seg))).astype(F32)
        c_ref[...] = _dot_exact(tri, lf)


def _logf_cumsum(z, bias, nb, t):
    m = nb * t
    if t >= 128:
        tr = _pick_tile(t, 704, 8)
        grid = (nb, t // tr)
        spec = pl.BlockSpec((tr, LANES), lambda b, i: (b * (t // tr) + i, 0))
    else:
        tr = m
        grid = (1, 1)
        spec = pl.BlockSpec((tr, LANES), lambda b, i: (0, 0))
    return pl.pallas_call(
        functools.partial(_logf_kernel, seg=t),
        grid=grid,
        in_specs=[spec, pl.BlockSpec((1, LANES), lambda b, i: (0, 0))],
        out_specs=[spec, spec],
        out_shape=[jax.ShapeDtypeStruct((m, LANES), F32)] * 2,
        scratch_shapes=[pltpu.VMEM((1, LANES), F32)],
        compiler_params=_cparams("arbitrary", "arbitrary"),
        name="logf_cumsum",
    )(z, bias)


def _fox_prompt_kernel(q_ref, k_ref, v_ref, g_ref, cm_ref, cs_ref, y_ref, kb_ref, vb_ref,
                       *, n_meta, tq):
    t_len = q_ref.shape[0]
    nblk = (t_len - n_meta) // tq
    kb_ref[...] = k_ref[...].astype(BF16)
    vb_ref[...] = v_ref[...].astype(BF16)
    km = kb_ref[0:n_meta, :]
    vm = vb_ref[0:n_meta, :]
    cm = cm_ref[...] * LOG2E
    cs = cs_ref[...] * LOG2E

    def emit(r0, n, acc, l):
        o = acc * (1.0 / l)
        y_ref[r0:r0 + n, :] = (o * _silu(g_ref[r0:r0 + n, :])).astype(y_ref.dtype)

    s = _dot_nt(q_ref[0:n_meta, :], km) - cm
    s = jnp.where(_iota2(s.shape, 1) <= _iota2(s.shape, 0), s, -jnp.inf)
    m = jnp.max(s, axis=1, keepdims=True)
    p = jnp.exp2(s - m)
    emit(0, n_meta, _dot(p.astype(BF16), vm), jnp.sum(p, axis=1, keepdims=True))

    for i in range(nblk):
        r0 = n_meta + i * tq
        q = q_ref[r0:r0 + tq, :]
        s0 = _dot_nt(q, km) - cm
        m_i = jnp.max(s0, axis=1, keepdims=True)
        p0 = jnp.exp2(s0 - m_i)
        l_i = jnp.broadcast_to(jnp.sum(p0, axis=1, keepdims=True) * (1.0 / LANES), (tq, LANES))
        acc = _dot(p0.astype(BF16), vm)
        for j in range(i + 1):
            k0 = n_meta + j * tq
            s_ = _dot_nt(q, kb_ref[k0:k0 + tq, :]) - cs[:, j * tq:(j + 1) * tq]
            if j == i:
                s_ = jnp.where(_iota2(s_.shape, 1) <= _iota2(s_.shape, 0), s_, -jnp.inf)
            m_n = jnp.maximum(m_i, jnp.max(s_, axis=1, keepdims=True))
            alpha = jnp.exp2(m_i - m_n)
            p_ = jnp.exp2(s_ - m_n)
            l_i = alpha * l_i + sum(p_[:, c0:c0 + LANES] for c0 in range(0, tq, LANES))
            acc = alpha * acc + _dot(p_.astype(BF16), vb_ref[k0:k0 + tq, :])
            m_i = m_n
        emit(r0, tq, acc, jnp.sum(l_i, axis=1, keepdims=True))


def _fox_prompt(q, k_all, v_all, g, c_meta, c_seq, layer, n_heads):
    b, t, w = q.shape
    dh = w // n_heads
    s_len = t - N_META
    tq = _pick_tile(s_len, 256, LANES)
    blk = lambda: pl.BlockSpec((None, t, dh), lambda bi, hi: (bi, 0, hi))
    lblk = lambda: pl.BlockSpec((None, None, t, dh), lambda bi, hi: (layer, bi, 0, hi))
    return pl.pallas_call(
        functools.partial(_fox_prompt_kernel, n_meta=N_META, tq=tq),
        grid=(b, n_heads),
        in_specs=[blk(), lblk(), lblk(), blk(),
                  pl.BlockSpec((None, None, 1, N_META), lambda bi, hi: (bi, hi, 0, 0)),
                  pl.BlockSpec((None, None, 1, s_len), lambda bi, hi: (bi, hi, 0, 0))],
        out_specs=blk(),
        out_shape=jax.ShapeDtypeStruct((b, t, w), BF16),
        scratch_shapes=[pltpu.VMEM((t, dh), BF16), pltpu.VMEM((t, dh), BF16)],
        compiler_params=_cparams("arbitrary", "arbitrary"),
        name="fox_prompt_attn",
    )(q, k_all, v_all, g, c_meta, c_seq)


SAMPLE_PAGES_PER_STEP = (4, 2, 1)


def _fox_sample_kernel(pt_ref, q_ref, g_ref, kn_ref, vn_ref, cn_ref, *rest, scale, pps):
    del pt_ref
    kp_refs, vp_refs, lfp_refs = rest[:pps], rest[pps:2 * pps], rest[2 * pps:3 * pps]
    y_ref, m_ref, l_ref, acc_ref, carry_ref, wq_ref, lr_ref, neg_ref = rest[3 * pps:]
    p_idx = pl.program_id(1)
    n_steps = pl.num_programs(1)
    page, n_heads, dh = kp_refs[0].shape
    n_new = kn_ref.shape[0]
    n_groups = n_heads // SUBLANES
    grp_rows = SUBLANES * n_new
    grp_keys = SUBLANES * page

    @pl.when(p_idx == 0)
    def _():
        m_ref[...] = jnp.full_like(m_ref, -jnp.inf)
        l_ref[...] = jnp.zeros_like(l_ref)
        acc_ref[...] = jnp.zeros_like(acc_ref)
        carry_ref[...] = jnp.zeros_like(carry_ref)
        wq_ref[...] = (q_ref[...] * (scale * LOG2E)).astype(BF16)
        lr_ref[...] = (_iota2(lr_ref.shape, 0) > _iota2(lr_ref.shape, 1) // SUBLANES).astype(F32)
        same_head = (_iota2(neg_ref.shape, 1) % SUBLANES) == (_iota2(neg_ref.shape, 0) // n_new)
        neg_ref[...] = jnp.where(same_head, 0.0, -jnp.inf)

    def update(state, s, vb):
        m_i, l_i, acc = state
        m_n = jnp.maximum(m_i, jnp.max(s, axis=1, keepdims=True))
        alpha = jnp.exp2(m_i - m_n)
        p = jnp.exp2(s - m_n)
        return (m_n, alpha * l_i + jnp.sum(p, axis=1, keepdims=True),
                alpha * acc + _dot(p.astype(BF16), vb))

    def load_state():
        m_all, l_all, acc_all = m_ref[...], l_ref[...], acc_ref[...]
        rows = [slice(g * grp_rows, (g + 1) * grp_rows) for g in range(n_groups)]
        return [(m_all[r, :], l_all[r, :], acc_all[r, :]) for r in rows]

    def store_state(states):
        for ref, parts in zip((m_ref, l_ref, acc_ref), zip(*states)):
            ref[...] = parts[0] if n_groups == 1 else jnp.concatenate(parts, axis=0)

    def group(ref, g):
        blk = ref[:, g * SUBLANES:(g + 1) * SUBLANES, :]
        return blk.reshape(blk.shape[0] * SUBLANES, dh).astype(BF16)

    carry = carry_ref[...]
    suffs = []
    for i in range(pps):
        lf = lfp_refs[i][...]
        suffs.append((_dot_exact(lf, lr_ref[...]) + carry) * LOG2E)
        carry = carry + jnp.sum(lf, axis=1, keepdims=True)
    carry_ref[...] = carry
    own_head = (_iota2((SUBLANES, grp_keys), 1) % SUBLANES) == _iota2((SUBLANES, grp_keys), 0)
    states = load_state()
    for g in range(n_groups):
        wq = wq_ref[g * grp_rows:(g + 1) * grp_rows, :]
        s_parts = []
        for i in range(pps):
            sg = suffs[i][g * SUBLANES:(g + 1) * SUBLANES, :]
            bias = jnp.sum(jnp.where(own_head, sg, 0.0), axis=0, keepdims=True)
            s_parts.append(_dot_nt(wq, group(kp_refs[i], g)) + (bias + neg_ref[...]))
        s_past = s_parts[0] if pps == 1 else jnp.concatenate(s_parts, axis=1)
        v_parts = [group(vp_refs[i], g) for i in range(pps)]
        states[g] = update(states[g], s_past, v_parts[0] if pps == 1 else jnp.concatenate(v_parts, axis=0))
    store_state(states)

    @pl.when(p_idx == n_steps - 1)
    def _():
        new_keys = SUBLANES * n_new
        r_ = _iota2((grp_rows, new_keys), 0)
        c_ = _iota2((grp_rows, new_keys), 1)
        valid = ((c_ % SUBLANES) == (r_ // n_new)) & ((c_ // SUBLANES) <= (r_ % n_new))
        final = load_state()
        for g in range(n_groups):
            wq = wq_ref[g * grp_rows:(g + 1) * grp_rows, :]
            s_new = _dot_nt(wq, group(kn_ref, g)) + cn_ref[g] * LOG2E
            final[g] = update(final[g], jnp.where(valid, s_new, -jnp.inf), group(vn_ref, g))
        o = jnp.concatenate([acc * (1.0 / l_f) for _, l_f, acc in final], axis=0)
        y_ref[...] = (o * _silu(g_ref[...])).astype(y_ref.dtype)


def _fox_sample(q, g_new, k_all, v_all, c_new, cache_k, cache_v, cache_lf_t, layer, page_table, n_heads):
    db, s_new, w = q.shape
    n_pages = page_table.shape[1]
    ps = cache_k.shape[2]
    dh = w // n_heads
    n_layers = k_all.shape[0]
    n_groups = n_heads // SUBLANES
    rows = s_new * n_heads
    to_rows = lambda a: a.reshape(db, s_new, n_heads, dh).transpose(0, 2, 1, 3).reshape(db, rows, dh)
    q_r, g_r = to_rows(q), to_rows(g_new)
    kn = k_all.reshape(n_layers, db, s_new, n_heads, dh)
    vn = v_all.reshape(n_layers, db, s_new, n_heads, dh)
    cn = -c_new.reshape(db, s_new, n_groups, SUBLANES).transpose(0, 2, 1, 3)
    cn = cn.reshape(db, n_groups, 1, s_new * SUBLANES)
    per_b = lambda: pl.BlockSpec((None, rows, dh), lambda b, p, pt: (b, 0, 0))
    new_spec = lambda: pl.BlockSpec((None, None, s_new, n_heads, dh), lambda b, p, pt: (layer, b, 0, 0, 0))
    pps = max(c for c in SAMPLE_PAGES_PER_STEP if n_pages % c == 0)

    def page_spec(i):
        return pl.BlockSpec(
            (None, None, ps, n_heads, dh),
            lambda b, p, pt: (layer, pt[b, n_pages - 1 - (p * pps + i)], 0, 0, 0))

    def lf_spec(i):
        return pl.BlockSpec(
            (None, None, n_heads, ps),
            lambda b, p, pt: (layer, pt[b, n_pages - 1 - (p * pps + i)], 0, 0))

    grid_spec = pltpu.PrefetchScalarGridSpec(
        num_scalar_prefetch=1,
        grid=(db, n_pages // pps),
        in_specs=[per_b(), per_b(), new_spec(), new_spec(),
                  pl.BlockSpec((None, n_groups, 1, s_new * SUBLANES), lambda b, p, pt: (b, 0, 0, 0))]
                 + [page_spec(i) for i in range(pps)] + [page_spec(i) for i in range(pps)]
                 + [lf_spec(i) for i in range(pps)],
        out_specs=per_b(),
        scratch_shapes=[pltpu.VMEM((rows, 1), F32), pltpu.VMEM((rows, 1), F32),
                        pltpu.VMEM((rows, dh), F32), pltpu.VMEM((n_heads, 1), F32),
                        pltpu.VMEM((rows, dh), BF16), pltpu.VMEM((ps, ps * SUBLANES), F32),
                        pltpu.VMEM((SUBLANES * s_new, ps * SUBLANES), F32)],
    )
    y = pl.pallas_call(
        functools.partial(_fox_sample_kernel, scale=dh ** -0.5, pps=pps),
        grid_spec=grid_spec,
        out_shape=jax.ShapeDtypeStruct((db, rows, dh), BF16),
        compiler_params=_cparams("arbitrary", "arbitrary"),
        name="fox_sample_attn",
    )(page_table, q_r, g_r, kn, vn, cn, *([cache_k] * pps), *([cache_v] * pps), *([cache_lf_t] * pps))
    return y.reshape(db, n_heads, s_new, dh).transpose(0, 2, 1, 3).reshape(db, s_new, w)


GLA_CHUNK = 128
GLA_UNROLL = 2


def _gla_chunk(r0, c, refs, wa, q_scale):
    q_ref, k_ref, v_ref, g_ref, a_ref, bg_ref, on_ref, y_ref, s_ref = refs
    rows = pl.ds(r0, c)
    q = q_ref[rows, :] * q_scale
    k = k_ref[rows, :]
    v = v_ref[rows, :].astype(BF16)
    dk = q.shape[1]
    x = _dot(a_ref[rows, :].astype(BF16), wa) + bg_ref[...]
    la = _log_sigmoid(x) * (LOG2E / GLA_GATE_TAU)
    row = _iota2((c, c), 0)
    col = _iota2((c, c), 1)
    b = _dot_exact((col <= row).astype(F32), la)
    b_last = b[c - 1:c, :]

    s_old = s_ref[...]
    o = _dot((q * jnp.exp2(b)).astype(BF16), s_old.astype(BF16))

    a_mat = jnp.where(row == col, jnp.sum(q * k, axis=1, keepdims=True), 0.0)
    xor = row ^ col
    sub = _iota2((c, dk), 0) % SUBLANES

    def tile_rows(first, stride, n):
        pieces = [jnp.broadcast_to(b[r:r + 1, :], (n, dk)) for r in range(first, c, stride)]
        return pieces[0] if len(pieces) == 1 else jnp.concatenate(pieces, axis=0)

    blk = 2
    while blk <= c:
        half = blk // 2
        if blk < SUBLANES:
            b_mid = tile_rows(half - 1, SUBLANES, SUBLANES)
            for s0 in range(blk, SUBLANES, blk):
                b_mid = jnp.where(sub >= s0, tile_rows(s0 + half - 1, SUBLANES, SUBLANES), b_mid)
        else:
            b_mid = tile_rows(half - 1, blk, blk)
        q_t = (q * jnp.exp2(b - b_mid)).astype(BF16)
        k_t = (k * jnp.exp2(b_mid - b)).astype(BF16)
        a_mat = jnp.where(xor >= half, _dot_nt(q_t, k_t), a_mat)
        blk *= 2
    a_mat = jnp.where(col <= row, a_mat, 0.0)
    o = o + _dot(a_mat.astype(BF16), v)

    k_end = (k * jnp.exp2(b_last - b)).astype(BF16)
    decay = jnp.transpose(jnp.broadcast_to(jnp.exp2(b_last), (8, dk)))[:, 0:1]
    s_ref[...] = decay * s_old + _dot_tn(k_end, v)

    o = o * lax.rsqrt(jnp.mean(o * o, axis=-1, keepdims=True) + EPS) * on_ref[...]
    y_ref[rows, :] = (o * _silu(g_ref[rows, :])).astype(y_ref.dtype)


def _gla_kernel(*refs, head, n_chunks, has_state, has_prev, q_scale):
    q_ref, k_ref, v_ref, g_ref, a_ref, wa_ref, bg_ref, on_ref = refs[:8]
    y_ref, so_ref, s_ref = refs[8 + has_state + has_prev:]
    if has_state:
        s_ref[...] = refs[8][...]
    else:
        s_ref[...] = jnp.zeros_like(s_ref)
    wa = wa_ref[...].astype(BF16)
    crefs = (q_ref, k_ref, v_ref, g_ref, a_ref, bg_ref, on_ref, y_ref, s_ref)
    if head:
        _gla_chunk(0, head, crefs, wa, q_scale)

    def body(i, carry):
        _gla_chunk(pl.multiple_of(head + i * GLA_CHUNK, 16), GLA_CHUNK, crefs, wa, q_scale)
        return carry

    if n_chunks:
        lax.fori_loop(0, n_chunks, body, 0, unroll=GLA_UNROLL if n_chunks % GLA_UNROLL == 0 else 1)
    so_ref[...] = s_ref[...]


def _gla(zq, zk, zv, zg, za, wa2, b_gate, onorm, state0, layer, n_layers, s_prev, n_heads):
    b, t, kw = zq.shape
    vw = zv.shape[2]
    dk, dv = kw // n_heads, vw // n_heads
    head = t % GLA_CHUNK
    n_chunks = t // GLA_CHUNK
    assert head in (0, 8, 16, 32, 64)
    tok = lambda last: pl.BlockSpec((None, t, last), lambda bi, hi: (bi, 0, hi))
    in_specs = [tok(dk), tok(dk), tok(dv), tok(dv),
                pl.BlockSpec((None, t, LANES), lambda bi, hi: (bi, 0, 0)),
                pl.BlockSpec((LANES, dk), lambda bi, hi: (0, hi)),
                pl.BlockSpec((1, dk), lambda bi, hi: (0, hi)),
                pl.BlockSpec((1, dv), lambda bi, hi: (0, 0))]
    args = [zq, zk, zv, zg, za, wa2, b_gate, onorm]
    if state0 is not None:
        in_specs.append(pl.BlockSpec((None, None, None, dk, dv), lambda bi, hi: (layer, bi, hi, 0, 0)))
        args.append(state0)
    aliases = {}
    if s_prev is not None:
        in_specs.append(pl.BlockSpec(memory_space=pl.ANY))
        aliases = {len(args): 1}
        args.append(s_prev)
    return pl.pallas_call(
        functools.partial(_gla_kernel, head=head, n_chunks=n_chunks, has_state=state0 is not None,
                          has_prev=s_prev is not None, q_scale=dk ** -0.5),
        grid=(b, n_heads),
        in_specs=in_specs,
        out_specs=[tok(dv),
                   pl.BlockSpec((None, None, None, dk, dv), lambda bi, hi: (layer, bi, hi, 0, 0))],
        out_shape=[jax.ShapeDtypeStruct((b, t, vw), BF16),
                   jax.ShapeDtypeStruct((n_layers, b, n_heads, dk, dv), F32)],
        scratch_shapes=[pltpu.VMEM((dk, dv), F32)],
        input_output_aliases=aliases,
        compiler_params=_cparams("arbitrary", "arbitrary"),
        name="gla",
    )(*args)


def _pad_rows(x):
    return jnp.pad(x, ((0, LANES - x.shape[0]), (0, 0)))


def _fox_layer(h_p, h_s, dims, norm, w_in_t, b_f, w_out, layer, n_heads, prev, sample_ctx):
    bsz, t, db, ds = dims
    d = h_p.shape[1]
    aw = w_out.shape[1]
    n_layers = w_out.shape[0]
    dh = aw // n_heads
    kp_prev, vp_prev, ks_prev, vs_prev = prev
    u_p = _rmsnorm(h_p, norm[layer], BF16)
    u_s = _rmsnorm(h_s, norm[layer], BF16)
    proj_in = functools.partial(_proj, u_p, u_s, w_in_t, layer, w_is_nk=True)
    q_p, q_s = proj_in(0, aw, BF16, out_scale=dh ** -0.5 * LOG2E)
    k_p, k_s = proj_in(aw, aw, F32, stack=(n_layers, kp_prev, ks_prev))
    v_p, v_s = proj_in(2 * aw, aw, F32, stack=(n_layers, vp_prev, vs_prev))
    g_p, g_s = proj_in(3 * aw, aw, F32)
    w_tail = _pad_rows(w_in_t[layer, 4 * aw:, :])
    bias = jnp.pad(b_f[layer][None, :].astype(F32), ((0, 0), (0, LANES - n_heads)))
    lf_p, c_p = _logf_cumsum(_tail_proj(u_p, w_tail), bias, bsz, t)
    lf_s, c_s = _logf_cumsum(_tail_proj(u_s, w_tail), bias, db, ds)
    lf_p = lf_p[:, :n_heads].reshape(bsz, t, n_heads)
    lf_s = lf_s[:, :n_heads].reshape(db, ds, n_heads)
    c_t = jnp.transpose(c_p[:, :n_heads].reshape(bsz, t, n_heads), (0, 2, 1))[:, :, None, :]
    y_p = _fox_prompt(q_p.reshape(bsz, t, aw), k_p.reshape(n_layers, bsz, t, aw),
                      v_p.reshape(n_layers, bsz, t, aw), g_p.reshape(bsz, t, aw),
                      c_t[..., :N_META], c_t[..., N_META:], layer, n_heads)
    cache_k, cache_v, cache_lf_t, page_table = sample_ctx
    y_s = _fox_sample(q_s.reshape(db, ds, aw), g_s.reshape(db, ds, aw), k_s, v_s,
                      c_s[:, :n_heads].reshape(db, ds, n_heads),
                      cache_k, cache_v, cache_lf_t, layer, page_table, n_heads)
    h_p, h_s = _proj(y_p.reshape(bsz * t, aw), y_s.reshape(db * ds, aw), w_out, layer, 0, d, F32,
                     w_is_nk=False, residual=(h_p, h_s))
    return h_p, h_s, (k_p, v_p, k_s, v_s), lf_p, lf_s


def _gla_layer(h_p, h_s, dims, norm, w_in_t, w_a2, b_gate, onorm, w_out, layer, n_heads, state0, prev):
    bsz, t, db, ds = dims
    d = h_p.shape[1]
    vw = w_out.shape[1]
    kw = w_a2.shape[2]
    n_layers = w_out.shape[0]
    u_p = _rmsnorm(h_p, norm[layer], BF16)
    u_s = _rmsnorm(h_s, norm[layer], BF16)
    proj_in = functools.partial(_proj, u_p, u_s, w_in_t, layer, w_is_nk=True)
    zq_p, zq_s = proj_in(0, kw, F32)
    zk_p, zk_s = proj_in(kw, kw, F32)
    zv_p, zv_s = proj_in(2 * kw, vw, F32)
    zg_p, zg_s = proj_in(2 * kw + vw, vw, F32)
    w_tail = _pad_rows(w_in_t[layer, 2 * kw + 2 * vw:, :])
    za_p, za_s = _tail_proj(u_p, w_tail), _tail_proj(u_s, w_tail)
    wa2 = _pad_rows(w_a2[layer])
    bg, on = b_gate[layer][None, :], onorm[layer][None, :]
    y_p, st_p = _gla(zq_p.reshape(bsz, t, kw), zk_p.reshape(bsz, t, kw), zv_p.reshape(bsz, t, vw),
                     zg_p.reshape(bsz, t, vw), za_p.reshape(bsz, t, LANES), wa2, bg, on,
                     None, layer, n_layers, prev[0], n_heads)
    y_s, st_s = _gla(zq_s.reshape(db, ds, kw), zk_s.reshape(db, ds, kw), zv_s.reshape(db, ds, vw),
                     zg_s.reshape(db, ds, vw), za_s.reshape(db, ds, LANES), wa2, bg, on,
                     state0, layer, n_layers, prev[1], n_heads)
    h_p, h_s = _proj(y_p.reshape(bsz * t, vw), y_s.reshape(db * ds, vw), w_out, layer, 0, d, F32,
                     w_is_nk=False, residual=(h_p, h_s))
    return h_p, h_s, (st_p, st_s)


def kernel(x_prompt, x_sample, cache_k, cache_v, cache_logf, state_gla, page_table, meta_tokens,
           norm_a, w_in_a, b_forget, w_out_a, norm_b, w_in_b, w_gate_up, b_gate, onorm_b, w_out_b,
           final_norm):
    bsz, seq, d = x_prompt.shape
    db, ds, _ = x_sample.shape
    a_heads = b_forget.shape[1]
    b_heads = state_gla.shape[2]
    dh = w_out_a.shape[1] // a_heads
    depth = norm_a.shape[0] + norm_b.shape[0]
    t = N_META + seq
    meta = jnp.broadcast_to(meta_tokens.astype(x_prompt.dtype)[None], (bsz, N_META, d))
    h_p = jnp.concatenate([meta, x_prompt], axis=1).reshape(bsz * t, d)
    h_s = x_sample.reshape(db * ds, d)
    w_in_a_t = jnp.swapaxes(w_in_a, 1, 2)
    w_in_b_t = jnp.swapaxes(w_in_b, 1, 2)
    cache_lf_t = jnp.swapaxes(cache_logf, 2, 3)
    dims = (bsz, t, db, ds)
    kv = (None, None, None, None)
    states = (None, None)
    nf_p, nf_s = [], []
    for i in range(depth):
        j = i // 2
        if i % 2 == 0:
            h_p, h_s, kv, lf_p, lf_s = _fox_layer(
                h_p, h_s, dims, norm_a, w_in_a_t, b_forget, w_out_a, j, a_heads, kv,
                (cache_k, cache_v, cache_lf_t, page_table))
            nf_p.append(lf_p)
            nf_s.append(lf_s)
        else:
            h_p, h_s, states = _gla_layer(
                h_p, h_s, dims, norm_b, w_in_b_t, w_gate_up, b_gate, onorm_b, w_out_b, j, b_heads,
                state_gla, states)
    y_prompt = _rmsnorm(h_p, final_norm, F32).reshape(bsz, t, d)[:, N_META:]
    y_sample = _rmsnorm(h_s, final_norm, F32).reshape(db, ds, d)
    n_a = norm_a.shape[0]
    kp_shape = (n_a, bsz, t, a_heads, dh)
    ks_shape = (n_a, db, ds, a_heads, dh)
    return (y_prompt, y_sample, kv[0].reshape(kp_shape), kv[1].reshape(kp_shape), jnp.stack(nf_p),
            kv[2].reshape(ks_shape), kv[3].reshape(ks_shape), jnp.stack(nf_s), states[0], states[1])
```

```python
import functools

import jax
import jax.numpy as jnp
from jax import lax
from jax.experimental import pallas as pl
from jax.experimental.pallas import tpu as pltpu

EPS = 1e-6
N_META = 16
GLA_GATE_TAU = 16.0
LOG2E = 1.4426950408889634
LANES = 128
SUBLANES = 8
VMEM_LIMIT_BYTES = 56 * 1024 * 1024
F32 = jnp.float32
BF16 = jnp.bfloat16
HIGHEST = lax.Precision.HIGHEST


def _cparams(*sem):
    return pltpu.CompilerParams(dimension_semantics=sem, vmem_limit_bytes=VMEM_LIMIT_BYTES)


def _pick_tile(n, target, mult):
    best = None
    for t in range(mult, min(n, target) + 1, mult):
        if n % t == 0:
            best = t
    return best if best is not None else n


def _dot(a, b):
    return jnp.dot(a, b, preferred_element_type=F32)


def _dot_nt(a, b, precision=None):
    return lax.dot_general(a, b, (((1,), (1,)), ((), ())), precision=precision,
                           preferred_element_type=F32)


def _dot_tn(a, b):
    return lax.dot_general(a, b, (((0,), (0,)), ((), ())), preferred_element_type=F32)


def _dot_exact(a, b):
    return jnp.dot(a, b, precision=HIGHEST, preferred_element_type=F32)


def _log_sigmoid(x):
    return jnp.minimum(x, 0.0) - jnp.log1p(jnp.exp(-jnp.abs(x)))


def _silu(x):
    return x * (1.0 / (1.0 + jnp.exp(-x)))


def _iota2(shape, dim):
    return lax.broadcasted_iota(jnp.int32, shape, dim)


def _rmsnorm_kernel(x_ref, g_ref, o_ref):
    x = x_ref[...]
    ms = jnp.mean(x * x, axis=-1, keepdims=True)
    o_ref[...] = (x * lax.rsqrt(ms + EPS) * g_ref[...]).astype(o_ref.dtype)


def _rmsnorm(x, g, out_dtype):
    m, d = x.shape
    tm = _pick_tile(m, 512, 16)
    return pl.pallas_call(
        _rmsnorm_kernel,
        grid=(m // tm,),
        in_specs=[pl.BlockSpec((tm, d), lambda i: (i, 0)),
                  pl.BlockSpec((1, d), lambda i: (0, 0))],
        out_specs=pl.BlockSpec((tm, d), lambda i: (i, 0)),
        out_shape=jax.ShapeDtypeStruct((m, d), out_dtype),
        compiler_params=_cparams("arbitrary"),
        name="rmsnorm",
    )(x, g.reshape(1, d))


def _proj_kernel(*refs, w_is_nk, has_res, n_prev, out_scale):
    x_ref, xs_ref, w_ref = refs[:3]
    r_ref, rs_ref = refs[3:5] if has_res else (None, None)
    o_ref, os_ref, wb_ref = refs[(5 if has_res else 3) + n_prev:]

    def mm(x):
        return _dot_nt(x, wb_ref[...]) if w_is_nk else _dot(x, wb_ref[...])

    @pl.when(pl.program_id(1) == 0)
    def _():
        wb_ref[...] = w_ref[...].astype(BF16)
        acc_s = mm(xs_ref[...])
        if has_res:
            acc_s = rs_ref[...] + acc_s
        os_ref[...] = acc_s.astype(os_ref.dtype)

    acc = mm(x_ref[...])
    if has_res:
        acc = r_ref[...] + acc
    if out_scale is not None:
        acc = acc * out_scale
    o_ref[...] = acc.astype(o_ref.dtype)


def _proj(x, xs, w, layer, col0, ncols, out_dtype, *, w_is_nk, residual=None, stack=None,
          out_scale=None, tn=512, tm_target=704):
    m, k = x.shape
    ms = xs.shape[0]
    tm = _pick_tile(m, tm_target, 16)
    assert ncols % tn == 0 and col0 % tn == 0
    cb = col0 // tn
    if w_is_nk:
        w_spec = pl.BlockSpec((None, tn, k), lambda n, i: (layer, cb + n, 0))
        wb_shape = (tn, k)
    else:
        w_spec = pl.BlockSpec((None, k, tn), lambda n, i: (layer, 0, cb + n))
        wb_shape = (k, tn)
    in_specs = [pl.BlockSpec((tm, k), lambda n, i: (i, 0)),
                pl.BlockSpec((ms, k), lambda n, i: (0, 0)), w_spec]
    args = [x, xs, w]
    aliases = {}
    if residual is not None:
        in_specs += [pl.BlockSpec((tm, tn), lambda n, i: (i, n)),
                     pl.BlockSpec((ms, tn), lambda n, i: (0, n))]
        args += list(residual)
        aliases = {3: 0, 4: 1}
    n_prev = 0
    if stack is None:
        out_specs = [pl.BlockSpec((tm, tn), lambda n, i: (i, n)),
                     pl.BlockSpec((ms, tn), lambda n, i: (0, n))]
        out_shape = [jax.ShapeDtypeStruct((m, ncols), out_dtype),
                     jax.ShapeDtypeStruct((ms, ncols), F32)]
    else:
        n_layers, prev, prev_s = stack
        out_specs = [pl.BlockSpec((None, tm, tn), lambda n, i: (layer, i, n)),
                     pl.BlockSpec((None, ms, tn), lambda n, i: (layer, 0, n))]
        out_shape = [jax.ShapeDtypeStruct((n_layers, m, ncols), out_dtype),
                     jax.ShapeDtypeStruct((n_layers, ms, ncols), F32)]
        if prev is not None:
            n_prev = 2
            in_specs += [pl.BlockSpec(memory_space=pl.ANY)] * 2
            aliases = {len(args): 0, len(args) + 1: 1}
            args += [prev, prev_s]
    return pl.pallas_call(
        functools.partial(_proj_kernel, w_is_nk=w_is_nk, has_res=residual is not None,
                          n_prev=n_prev, out_scale=out_scale),
        grid=(ncols // tn, m // tm),
        in_specs=in_specs,
        out_specs=out_specs,
        out_shape=out_shape,
        scratch_shapes=[pltpu.VMEM(wb_shape, BF16)],
        input_output_aliases=aliases,
        compiler_params=_cparams("arbitrary", "arbitrary"),
        name="proj",
    )(*args)


def _tail_kernel(x_ref, w_ref, o_ref):
    o_ref[...] = _dot_nt(x_ref[...], w_ref[...].astype(BF16))


def _tail_proj(x, w_tail):
    m, k = x.shape
    tm = _pick_tile(m, 704, 16)
    return pl.pallas_call(
        _tail_kernel,
        grid=(m // tm,),
        in_specs=[pl.BlockSpec((tm, k), lambda i: (i, 0)),
                  pl.BlockSpec((LANES, k), lambda i: (0, 0))],
        out_specs=pl.BlockSpec((tm, LANES), lambda i: (i, 0)),
        out_shape=jax.ShapeDtypeStruct((m, LANES), F32),
        compiler_params=_cparams("arbitrary"),
        name="tail_proj",
    )(x, w_tail)


def _logf_kernel(z_ref, b_ref, lf_ref, c_ref, carry_ref, *, seg):
    tr = z_ref.shape[0]
    lf = _log_sigmoid(z_ref[...] + b_ref[...])
    lf_ref[...] = lf
    row = _iota2((tr, tr), 0)
    col = _iota2((tr, tr), 1)
    if seg >= tr:
        @pl.when(pl.program_id(1) == 0)
        def _():
            carry_ref[...] = jnp.zeros_like(carry_ref)

        tri = (col <= row).astype(F32)
        c = _dot_exact(tri, lf) + carry_ref[...]
        c_ref[...] = c
        carry_ref[...] = c[tr - 1:tr, :]
    else:
        tri = ((col <= row) & ((row // seg) == (col // seg))).astype(F32)
        c_ref[...] = _dot_exact(tri, lf)


def _logf_cumsum(z, bias, nb, t):
    m = nb * t
    if t >= 128:
        tr = _pick_tile(t, 704, 8)
        grid = (nb, t // tr)
        spec = pl.BlockSpec((tr, LANES), lambda b, i: (b * (t // tr) + i, 0))
    else:
        tr = m
        grid = (1, 1)
        spec = pl.BlockSpec((tr, LANES), lambda b, i: (0, 0))
    return pl.pallas_call(
        functools.partial(_logf_kernel, seg=t),
        grid=grid,
        in_specs=[spec, pl.BlockSpec((1, LANES), lambda b, i: (0, 0))],
        out_specs=[spec, spec],
        out_shape=[jax.ShapeDtypeStruct((m, LANES), F32)] * 2,
        scratch_shapes=[pltpu.VMEM((1, LANES), F32)],
        compiler_params=_cparams("arbitrary", "arbitrary"),
        name="logf_cumsum",
    )(z, bias)


def _fox_prompt_kernel(q_ref, k_ref, v_ref, g_ref, cm_ref, cs_ref, y_ref, kb_ref, vb_ref,
                       *, n_meta, tq):
    t_len = q_ref.shape[0]
    nblk = (t_len - n_meta) // tq
    kb_ref[...] = k_ref[...].astype(BF16)
    vb_ref[...] = v_ref[...].astype(BF16)
    km = kb_ref[0:n_meta, :]
    vm = vb_ref[0:n_meta, :]
    cm = cm_ref[...] * LOG2E
    cs = cs_ref[...] * LOG2E

    def emit(r0, n, acc, l):
        o = acc * (1.0 / l)
        y_ref[r0:r0 + n, :] = (o * _silu(g_ref[r0:r0 + n, :])).astype(y_ref.dtype)

    s = _dot_nt(q_ref[0:n_meta, :], km) - cm
    s = jnp.where(_iota2(s.shape, 1) <= _iota2(s.shape, 0), s, -jnp.inf)
    m = jnp.max(s, axis=1, keepdims=True)
    p = jnp.exp2(s - m)
    emit(0, n_meta, _dot(p.astype(BF16), vm), jnp.sum(p, axis=1, keepdims=True))

    for i in range(nblk):
        r0 = n_meta + i * tq
        q = q_ref[r0:r0 + tq, :]
        s0 = _dot_nt(q, km) - cm
        m_i = jnp.max(s0, axis=1, keepdims=True)
        p0 = jnp.exp2(s0 - m_i)
        l_i = jnp.broadcast_to(jnp.sum(p0, axis=1, keepdims=True) * (1.0 / LANES), (tq, LANES))
        acc = _dot(p0.astype(BF16), vm)
        for j in range(i + 1):
            k0 = n_meta + j * tq
            s_ = _dot_nt(q, kb_ref[k0:k0 + tq, :]) - cs[:, j * tq:(j + 1) * tq]
            if j == i:
                s_ = jnp.where(_iota2(s_.shape, 1) <= _iota2(s_.shape, 0), s_, -jnp.inf)
            m_n = jnp.maximum(m_i, jnp.max(s_, axis=1, keepdims=True))
            alpha = jnp.exp2(m_i - m_n)
            p_ = jnp.exp2(s_ - m_n)
            l_i = alpha * l_i + sum(p_[:, c0:c0 + LANES] for c0 in range(0, tq, LANES))
            acc = alpha * acc + _dot(p_.astype(BF16), vb_ref[k0:k0 + tq, :])
            m_i = m_n
        emit(r0, tq, acc, jnp.sum(l_i, axis=1, keepdims=True))


def _fox_prompt(q, k_all, v_all, g, c_meta, c_seq, layer, n_heads):
    b, t, w = q.shape
    dh = w // n_heads
    s_len = t - N_META
    tq = _pick_tile(s_len, 256, LANES)
    blk = lambda: pl.BlockSpec((None, t, dh), lambda bi, hi: (bi, 0, hi))
    lblk = lambda: pl.BlockSpec((None, None, t, dh), lambda bi, hi: (layer, bi, 0, hi))
    return pl.pallas_call(
        functools.partial(_fox_prompt_kernel, n_meta=N_META, tq=tq),
        grid=(b, n_heads),
        in_specs=[blk(), lblk(), lblk(), blk(),
                  pl.BlockSpec((None, None, 1, N_META), lambda bi, hi: (bi, hi, 0, 0)),
                  pl.BlockSpec((None, None, 1, s_len), lambda bi, hi: (bi, hi, 0, 0))],
        out_specs=blk(),
        out_shape=jax.ShapeDtypeStruct((b, t, w), BF16),
        scratch_shapes=[pltpu.VMEM((t, dh), BF16), pltpu.VMEM((t, dh), BF16)],
        compiler_params=_cparams("arbitrary", "arbitrary"),
        name="fox_prompt_attn",
    )(q, k_all, v_all, g, c_meta, c_seq)


SAMPLE_PAGES_PER_STEP = (4, 2, 1)


def _fox_sample_kernel(pt_ref, q_ref, g_ref, kn_ref, vn_ref, cn_ref, *rest, scale, pps):
    del pt_ref
    kp_refs, vp_refs, lfp_refs = rest[:pps], rest[pps:2 * pps], rest[2 * pps:3 * pps]
    y_ref, m_ref, l_ref, acc_ref, carry_ref, wq_ref, lr_ref, neg_ref = rest[3 * pps:]
    p_idx = pl.program_id(1)
    n_steps = pl.num_programs(1)
    page, n_heads, dh = kp_refs[0].shape
    n_new = kn_ref.shape[0]
    n_groups = n_heads // SUBLANES
    grp_rows = SUBLANES * n_new
    grp_keys = SUBLANES * page

    @pl.when(p_idx == 0)
    def _():
        m_ref[...] = jnp.full_like(m_ref, -jnp.inf)
        l_ref[...] = jnp.zeros_like(l_ref)
        acc_ref[...] = jnp.zeros_like(acc_ref)
        carry_ref[...] = jnp.zeros_like(carry_ref)
        wq_ref[...] = (q_ref[...] * (scale * LOG2E)).astype(BF16)
        lr_ref[...] = (_iota2(lr_ref.shape, 0) > _iota2(lr_ref.shape, 1) // SUBLANES).astype(F32)
        same_head = (_iota2(neg_ref.shape, 1) % SUBLANES) == (_iota2(neg_ref.shape, 0) // n_new)
        neg_ref[...] = jnp.where(same_head, 0.0, -jnp.inf)

    def update(state, s, vb):
        m_i, l_i, acc = state
        m_n = jnp.maximum(m_i, jnp.max(s, axis=1, keepdims=True))
        alpha = jnp.exp2(m_i - m_n)
        p = jnp.exp2(s - m_n)
        return (m_n, alpha * l_i + jnp.sum(p, axis=1, keepdims=True),
                alpha * acc + _dot(p.astype(BF16), vb))

    def load_state():
        m_all, l_all, acc_all = m_ref[...], l_ref[...], acc_ref[...]
        rows = [slice(g * grp_rows, (g + 1) * grp_rows) for g in range(n_groups)]
        return [(m_all[r, :], l_all[r, :], acc_all[r, :]) for r in rows]

    def store_state(states):
        for ref, parts in zip((m_ref, l_ref, acc_ref), zip(*states)):
            ref[...] = parts[0] if n_groups == 1 else jnp.concatenate(parts, axis=0)

    def group(ref, g):
        blk = ref[:, g * SUBLANES:(g + 1) * SUBLANES, :]
        return blk.reshape(blk.shape[0] * SUBLANES, dh).astype(BF16)

    carry = carry_ref[...]
    suffs = []
    for i in range(pps):
        lf = lfp_refs[i][...]
        suffs.append((_dot_exact(lf, lr_ref[...]) + carry) * LOG2E)
        carry = carry + jnp.sum(lf, axis=1, keepdims=True)
    carry_ref[...] = carry
    own_head = (_iota2((SUBLANES, grp_keys), 1) % SUBLANES) == _iota2((SUBLANES, grp_keys), 0)
    states = load_state()
    for g in range(n_groups):
        wq = wq_ref[g * grp_rows:(g + 1) * grp_rows, :]
        s_parts = []
        for i in range(pps):
            sg = suffs[i][g * SUBLANES:(g + 1) * SUBLANES, :]
            bias = jnp.sum(jnp.where(own_head, sg, 0.0), axis=0, keepdims=True)
            s_parts.append(_dot_nt(wq, group(kp_refs[i], g)) + (bias + neg_ref[...]))
        s_past = s_parts[0] if pps == 1 else jnp.concatenate(s_parts, axis=1)
        v_parts = [group(vp_refs[i], g) for i in range(pps)]
        states[g] = update(states[g], s_past, v_parts[0] if pps == 1 else jnp.concatenate(v_parts, axis=0))
    store_state(states)

    @pl.when(p_idx == n_steps - 1)
    def _():
        new_keys = SUBLANES * n_new
        r_ = _iota2((grp_rows, new_keys), 0)
        c_ = _iota2((grp_rows, new_keys), 1)
        valid = ((c_ % SUBLANES) == (r_ // n_new)) & ((c_ // SUBLANES) <= (r_ % n_new))
        final = load_state()
        for g in range(n_groups):
            wq = wq_ref[g * grp_rows:(g + 1) * grp_rows, :]
            s_new = _dot_nt(wq, group(kn_ref, g)) + cn_ref[g] * LOG2E
            final[g] = update(final[g], jnp.where(valid, s_new, -jnp.inf), group(vn_ref, g))
        o = jnp.concatenate([acc * (1.0 / l_f) for _, l_f, acc in final], axis=0)
        y_ref[...] = (o * _silu(g_ref[...])).astype(y_ref.dtype)


def _fox_sample(q, g_new, k_all, v_all, c_new, cache_k, cache_v, cache_lf_t, layer, page_table, n_heads):
    db, s_new, w = q.shape
    n_pages = page_table.shape[1]
    ps = cache_k.shape[2]
    dh = w // n_heads
    n_layers = k_all.shape[0]
    n_groups = n_heads // SUBLANES
    rows = s_new * n_heads
    to_rows = lambda a: a.reshape(db, s_new, n_heads, dh).transpose(0, 2, 1, 3).reshape(db, rows, dh)
    q_r, g_r = to_rows(q), to_rows(g_new)
    kn = k_all.reshape(n_layers, db, s_new, n_heads, dh)
    vn = v_all.reshape(n_layers, db, s_new, n_heads, dh)
    cn = -c_new.reshape(db, s_new, n_groups, SUBLANES).transpose(0, 2, 1, 3)
    cn = cn.reshape(db, n_groups, 1, s_new * SUBLANES)
    per_b = lambda: pl.BlockSpec((None, rows, dh), lambda b, p, pt: (b, 0, 0))
    new_spec = lambda: pl.BlockSpec((None, None, s_new, n_heads, dh), lambda b, p, pt: (layer, b, 0, 0, 0))
    pps = max(c for c in SAMPLE_PAGES_PER_STEP if n_pages % c == 0)

    def page_spec(i):
        return pl.BlockSpec(
            (None, None, ps, n_heads, dh),
            lambda b, p, pt: (layer, pt[b, n_pages - 1 - (p * pps + i)], 0, 0, 0))

    def lf_spec(i):
        return pl.BlockSpec(
            (None, None, n_heads, ps),
            lambda b, p, pt: (layer, pt[b, n_pages - 1 - (p * pps + i)], 0, 0))

    grid_spec = pltpu.PrefetchScalarGridSpec(
        num_scalar_prefetch=1,
        grid=(db, n_pages // pps),
        in_specs=[per_b(), per_b(), new_spec(), new_spec(),
                  pl.BlockSpec((None, n_groups, 1, s_new * SUBLANES), lambda b, p, pt: (b, 0, 0, 0))]
                 + [page_spec(i) for i in range(pps)] + [page_spec(i) for i in range(pps)]
                 + [lf_spec(i) for i in range(pps)],
        out_specs=per_b(),
        scratch_shapes=[pltpu.VMEM((rows, 1), F32), pltpu.VMEM((rows, 1), F32),
                        pltpu.VMEM((rows, dh), F32), pltpu.VMEM((n_heads, 1), F32),
                        pltpu.VMEM((rows, dh), BF16), pltpu.VMEM((ps, ps * SUBLANES), F32),
                        pltpu.VMEM((SUBLANES * s_new, ps * SUBLANES), F32)],
    )
    y = pl.pallas_call(
        functools.partial(_fox_sample_kernel, scale=dh ** -0.5, pps=pps),
        grid_spec=grid_spec,
        out_shape=jax.ShapeDtypeStruct((db, rows, dh), BF16),
        compiler_params=_cparams("arbitrary", "arbitrary"),
        name="fox_sample_attn",
    )(page_table, q_r, g_r, kn, vn, cn, *([cache_k] * pps), *([cache_v] * pps), *([cache_lf_t] * pps))
    return y.reshape(db, n_heads, s_new, dh).transpose(0, 2, 1, 3).reshape(db, s_new, w)


GLA_CHUNK = 256
GLA_UNROLL = 2


def _gla_chunk(r0, c, refs, wa, q_scale):
    q_ref, k_ref, v_ref, g_ref, a_ref, bg_ref, on_ref, y_ref, s_ref = refs
    rows = pl.ds(r0, c)
    q = q_ref[rows, :] * q_scale
    k = k_ref[rows, :]
    v = v_ref[rows, :].astype(BF16)
    dk = q.shape[1]
    x = _dot(a_ref[rows, :].astype(BF16), wa) + bg_ref[...]
    la = _log_sigmoid(x) * (LOG2E / GLA_GATE_TAU)
    row = _iota2((c, c), 0)
    col = _iota2((c, c), 1)
    b = _dot_exact((col <= row).astype(F32), la)
    b_last = b[c - 1:c, :]

    s_old = s_ref[...]
    o = _dot((q * jnp.exp2(b)).astype(BF16), s_old.astype(BF16))

    a_mat = jnp.where(row == col, jnp.sum(q * k, axis=1, keepdims=True), 0.0)
    xor = row ^ col
    sub = _iota2((c, dk), 0) % SUBLANES

    def tile_rows(first, stride, n):
        pieces = [jnp.broadcast_to(b[r:r + 1, :], (n, dk)) for r in range(first, c, stride)]
        return pieces[0] if len(pieces) == 1 else jnp.concatenate(pieces, axis=0)

    blk = 2
    while blk <= c:
        half = blk // 2
        if blk < SUBLANES:
            b_mid = tile_rows(half - 1, SUBLANES, SUBLANES)
            for s0 in range(blk, SUBLANES, blk):
                b_mid = jnp.where(sub >= s0, tile_rows(s0 + half - 1, SUBLANES, SUBLANES), b_mid)
        else:
            b_mid = tile_rows(half - 1, blk, blk)
        q_t = (q * jnp.exp2(b - b_mid)).astype(BF16)
        k_t = (k * jnp.exp2(b_mid - b)).astype(BF16)
        a_mat = jnp.where(xor >= half, _dot_nt(q_t, k_t), a_mat)
        blk *= 2
    a_mat = jnp.where(col <= row, a_mat, 0.0)
    o = o + _dot(a_mat.astype(BF16), v)

    k_end = (k * jnp.exp2(b_last - b)).astype(BF16)
    decay = jnp.transpose(jnp.broadcast_to(jnp.exp2(b_last), (8, dk)))[:, 0:1]
    s_ref[...] = decay * s_old + _dot_tn(k_end, v)

    o = o * lax.rsqrt(jnp.mean(o * o, axis=-1, keepdims=True) + EPS) * on_ref[...]
    y_ref[rows, :] = (o * _silu(g_ref[rows, :])).astype(y_ref.dtype)


def _gla_kernel(*refs, head, n_chunks, has_state, has_prev, q_scale):
    q_ref, k_ref, v_ref, g_ref, a_ref, wa_ref, bg_ref, on_ref = refs[:8]
    y_ref, so_ref, s_ref = refs[8 + has_state + has_prev:]
    if has_state:
        s_ref[...] = refs[8][...]
    else:
        s_ref[...] = jnp.zeros_like(s_ref)
    wa = wa_ref[...].astype(BF16)
    crefs = (q_ref, k_ref, v_ref, g_ref, a_ref, bg_ref, on_ref, y_ref, s_ref)
    if head:
        _gla_chunk(0, head, crefs, wa, q_scale)

    def body(i, carry):
        _gla_chunk(pl.multiple_of(head + i * GLA_CHUNK, 16), GLA_CHUNK, crefs, wa, q_scale)
        return carry

    if n_chunks:
        lax.fori_loop(0, n_chunks, body, 0, unroll=GLA_UNROLL if n_chunks % GLA_UNROLL == 0 else 1)
    so_ref[...] = s_ref[...]


def _gla(zq, zk, zv, zg, za, wa2, b_gate, onorm, state0, layer, n_layers, s_prev, n_heads):
    b, t, kw = zq.shape
    vw = zv.shape[2]
    dk, dv = kw // n_heads, vw // n_heads
    head = t % GLA_CHUNK
    n_chunks = t // GLA_CHUNK
    assert head in (0, 8, 16, 32, 64)
    tok = lambda last: pl.BlockSpec((None, t, last), lambda bi, hi: (bi, 0, hi))
    in_specs = [tok(dk), tok(dk), tok(dv), tok(dv),
                pl.BlockSpec((None, t, LANES), lambda bi, hi: (bi, 0, 0)),
                pl.BlockSpec((LANES, dk), lambda bi, hi: (0, hi)),
                pl.BlockSpec((1, dk), lambda bi, hi: (0, hi)),
                pl.BlockSpec((1, dv), lambda bi, hi: (0, 0))]
    args = [zq, zk, zv, zg, za, wa2, b_gate, onorm]
    if state0 is not None:
        in_specs.append(pl.BlockSpec((None, None, None, dk, dv), lambda bi, hi: (layer, bi, hi, 0, 0)))
        args.append(state0)
    aliases = {}
    if s_prev is not None:
        in_specs.append(pl.BlockSpec(memory_space=pl.ANY))
        aliases = {len(args): 1}
        args.append(s_prev)
    return pl.pallas_call(
        functools.partial(_gla_kernel, head=head, n_chunks=n_chunks, has_state=state0 is not None,
                          has_prev=s_prev is not None, q_scale=dk ** -0.5),
        grid=(b, n_heads),
        in_specs=in_specs,
        out_specs=[tok(dv),
                   pl.BlockSpec((None, None, None, dk, dv), lambda bi, hi: (layer, bi, hi, 0, 0))],
        out_shape=[jax.ShapeDtypeStruct((b, t, vw), BF16),
                   jax.ShapeDtypeStruct((n_layers, b, n_heads, dk, dv), F32)],
        scratch_shapes=[pltpu.VMEM((dk, dv), F32)],
        input_output_aliases=aliases,
        compiler_params=_cparams("arbitrary", "arbitrary"),
        name="gla",
    )(*args)


def _pad_rows(x):
    return jnp.pad(x, ((0, LANES - x.shape[0]), (0, 0)))


def _fox_layer(h_p, h_s, dims, norm, w_in_t, b_f, w_out, layer, n_heads, prev, sample_ctx):
    bsz, t, db, ds = dims
    d = h_p.shape[1]
    aw = w_out.shape[1]
    n_layers = w_out.shape[0]
    dh = aw // n_heads
    kp_prev, vp_prev, ks_prev, vs_prev = prev
    u_p = _rmsnorm(h_p, norm[layer], BF16)
    u_s = _rmsnorm(h_s, norm[layer], BF16)
    proj_in = functools.partial(_proj, u_p, u_s, w_in_t, layer, w_is_nk=True)
    q_p, q_s = proj_in(0, aw, BF16, out_scale=dh ** -0.5 * LOG2E)
    k_p, k_s = proj_in(aw, aw, F32, stack=(n_layers, kp_prev, ks_prev))
    v_p, v_s = proj_in(2 * aw, aw, F32, stack=(n_layers, vp_prev, vs_prev))
    g_p, g_s = proj_in(3 * aw, aw, F32)
    w_tail = _pad_rows(w_in_t[layer, 4 * aw:, :])
    bias = jnp.pad(b_f[layer][None, :].astype(F32), ((0, 0), (0, LANES - n_heads)))
    lf_p, c_p = _logf_cumsum(_tail_proj(u_p, w_tail), bias, bsz, t)
    lf_s, c_s = _logf_cumsum(_tail_proj(u_s, w_tail), bias, db, ds)
    lf_p = lf_p[:, :n_heads].reshape(bsz, t, n_heads)
    lf_s = lf_s[:, :n_heads].reshape(db, ds, n_heads)
    c_t = jnp.transpose(c_p[:, :n_heads].reshape(bsz, t, n_heads), (0, 2, 1))[:, :, None, :]
    y_p = _fox_prompt(q_p.reshape(bsz, t, aw), k_p.reshape(n_layers, bsz, t, aw),
                      v_p.reshape(n_layers, bsz, t, aw), g_p.reshape(bsz, t, aw),
                      c_t[..., :N_META], c_t[..., N_META:], layer, n_heads)
    cache_k, cache_v, cache_lf_t, page_table = sample_ctx
    y_s = _fox_sample(q_s.reshape(db, ds, aw), g_s.reshape(db, ds, aw), k_s, v_s,
                      c_s[:, :n_heads].reshape(db, ds, n_heads),
                      cache_k, cache_v, cache_lf_t, layer, page_table, n_heads)
    h_p, h_s = _proj(y_p.reshape(bsz * t, aw), y_s.reshape(db * ds, aw), w_out, layer, 0, d, F32,
                     w_is_nk=False, residual=(h_p, h_s))
    return h_p, h_s, (k_p, v_p, k_s, v_s), lf_p, lf_s


def _gla_layer(h_p, h_s, dims, norm, w_in_t, w_a2, b_gate, onorm, w_out, layer, n_heads, state0, prev):
    bsz, t, db, ds = dims
    d = h_p.shape[1]
    vw = w_out.shape[1]
    kw = w_a2.shape[2]
    n_layers = w_out.shape[0]
    u_p = _rmsnorm(h_p, norm[layer], BF16)
    u_s = _rmsnorm(h_s, norm[layer], BF16)
    proj_in = functools.partial(_proj, u_p, u_s, w_in_t, layer, w_is_nk=True)
    zq_p, zq_s = proj_in(0, kw, F32)
    zk_p, zk_s = proj_in(kw, kw, F32)
    zv_p, zv_s = proj_in(2 * kw, vw, F32)
    zg_p, zg_s = proj_in(2 * kw + vw, vw, F32)
    w_tail = _pad_rows(w_in_t[layer, 2 * kw + 2 * vw:, :])
    za_p, za_s = _tail_proj(u_p, w_tail), _tail_proj(u_s, w_tail)
    wa2 = _pad_rows(w_a2[layer])
    bg, on = b_gate[layer][None, :], onorm[layer][None, :]
    y_p, st_p = _gla(zq_p.reshape(bsz, t, kw), zk_p.reshape(bsz, t, kw), zv_p.reshape(bsz, t, vw),
                     zg_p.reshape(bsz, t, vw), za_p.reshape(bsz, t, LANES), wa2, bg, on,
                     None, layer, n_layers, prev[0], n_heads)
    y_s, st_s = _gla(zq_s.reshape(db, ds, kw), zk_s.reshape(db, ds, kw), zv_s.reshape(db, ds, vw),
                     zg_s.reshape(db, ds, vw), za_s.reshape(db, ds, LANES), wa2, bg, on,
                     state0, layer, n_layers, prev[1], n_heads)
    h_p, h_s = _proj(y_p.reshape(bsz * t, vw), y_s.reshape(db * ds, vw), w_out, layer, 0, d, F32,
                     w_is_nk=False, residual=(h_p, h_s))
    return h_p, h_s, (st_p, st_s)


def kernel(x_prompt, x_sample, cache_k, cache_v, cache_logf, state_gla, page_table, meta_tokens,
           norm_a, w_in_a, b_forget, w_out_a, norm_b, w_in_b, w_gate_up, b_gate, onorm_b, w_out_b,
           final_norm):
    bsz, seq, d = x_prompt.shape
    db, ds, _ = x_sample.shape
    a_heads = b_forget.shape[1]
    b_heads = state_gla.shape[2]
    dh = w_out_a.shape[1] // a_heads
    depth = norm_a.shape[0] + norm_b.shape[0]
    t = N_META + seq
    meta = jnp.broadcast_to(meta_tokens.astype(x_prompt.dtype)[None], (bsz, N_META, d))
    h_p = jnp.concatenate([meta, x_prompt], axis=1).reshape(bsz * t, d)
    h_s = x_sample.reshape(db * ds, d)
    w_in_a_t = jnp.swapaxes(w_in_a, 1, 2)
    w_in_b_t = jnp.swapaxes(w_in_b, 1, 2)
    cache_lf_t = jnp.swapaxes(cache_logf, 2, 3)
    dims = (bsz, t, db, ds)
    kv = (None, None, None, None)
    states = (None, None)
    nf_p, nf_s = [], []
    for i in range(depth):
        j = i // 2
        if i % 2 == 0:
            h_p, h_s, kv, lf_p, lf_s = _fox_layer(
                h_p, h_s, dims, norm_a, w_in_a_t, b_forget, w_out_a, j, a_heads, kv,
                (cache_k, cache_v, cache_lf_t, page_table))
            nf_p.append(lf_p)
            nf_s.append(lf_s)
        else:
            h_p, h_s, states = _gla_layer(
                h_p, h_s, dims, norm_b, w_in_b_t, w_gate_up, b_gate, onorm_b, w_out_b, j, b_heads,
                state_gla, states)
    y_prompt = _rmsnorm(h_p, final_norm, F32).reshape(bsz, t, d)[:, N_META:]
    y_sample = _rmsnorm(h_s, final_norm, F32).reshape(db, ds, d)
    n_a = norm_a.shape[0]
    kp_shape = (n_a, bsz, t, a_heads, dh)
    ks_shape = (n_a, db, ds, a_heads, dh)
    return (y_prompt, y_sample, kv[0].reshape(kp_shape), kv[1].reshape(kp_shape), jnp.stack(nf_p),
            kv[2].reshape(ks_shape), kv[3].reshape(ks_shape), jnp.stack(nf_s), states[0], states[1])
```

```python
import functools

import jax
import jax.numpy as jnp
from jax import lax
from jax.experimental import pallas as pl
from jax.experimental.pallas import tpu as pltpu

EPS = 1e-6
N_META = 16
GLA_GATE_TAU = 16.0
LOG2E = 1.4426950408889634
LANES = 128
SUBLANES = 8
VMEM_LIMIT_BYTES = 56 * 1024 * 1024
F32 = jnp.float32
BF16 = jnp.bfloat16


def _cparams(*sem):
    return pltpu.CompilerParams(dimension_semantics=sem, vmem_limit_bytes=VMEM_LIMIT_BYTES)


def _pick_tile(n, target, mult):
    best = None
    for t in range(mult, min(n, target) + 1, mult):
        if n % t == 0:
            best = t
    return best if best is not None else n


def _dot(a, b):
    return jnp.dot(a, b, preferred_element_type=F32)


def _dot_nt(a, b):
    return lax.dot_general(a, b, (((1,), (1,)), ((), ())), preferred_element_type=F32)


def _dot_tn(a, b):
    return lax.dot_general(a, b, (((0,), (0,)), ((), ())), preferred_element_type=F32)


def _split3(x):
    hi = x.astype(BF16)
    r1 = x - hi.astype(F32)
    mid = r1.astype(BF16)
    lo = (r1 - mid.astype(F32)).astype(BF16)
    return hi, mid, lo


def _sel_dot(sel, x):
    return sum(_dot(sel, part) for part in _split3(x))


def _dot_sel(x, sel):
    return sum(_dot(part, sel) for part in _split3(x))


def _log_sigmoid(x):
    return jnp.minimum(x, 0.0) - jnp.log1p(jnp.exp(-jnp.abs(x)))


def _silu(x):
    return x * (1.0 / (1.0 + jnp.exp(-x)))


def _iota2(shape, dim):
    return lax.broadcasted_iota(jnp.int32, shape, dim)


def _rmsnorm_kernel(x_ref, g_ref, o_ref):
    x = x_ref[...]
    ms = jnp.mean(x * x, axis=-1, keepdims=True)
    o_ref[...] = (x * lax.rsqrt(ms + EPS) * g_ref[...]).astype(o_ref.dtype)


def _rmsnorm(x, g, out_dtype):
    m, d = x.shape
    tm = _pick_tile(m, 512, 16)
    return pl.pallas_call(
        _rmsnorm_kernel,
        grid=(m // tm,),
        in_specs=[pl.BlockSpec((tm, d), lambda i: (i, 0)),
                  pl.BlockSpec((1, d), lambda i: (0, 0))],
        out_specs=pl.BlockSpec((tm, d), lambda i: (i, 0)),
        out_shape=jax.ShapeDtypeStruct((m, d), out_dtype),
        compiler_params=_cparams("arbitrary"),
        name="rmsnorm",
    )(x, g.reshape(1, d))


def _proj_kernel(*refs, w_is_nk, has_res, n_prev, out_scale):
    x_ref, xs_ref, w_ref = refs[:3]
    r_ref, rs_ref = refs[3:5] if has_res else (None, None)
    o_ref, os_ref, wb_ref = refs[(5 if has_res else 3) + n_prev:]

    def mm(x):
        return _dot_nt(x, wb_ref[...]) if w_is_nk else _dot(x, wb_ref[...])

    @pl.when(pl.program_id(1) == 0)
    def _():
        wb_ref[...] = w_ref[...].astype(BF16)
        acc_s = mm(xs_ref[...])
        if has_res:
            acc_s = rs_ref[...] + acc_s
        os_ref[...] = acc_s.astype(os_ref.dtype)

    acc = mm(x_ref[...])
    if has_res:
        acc = r_ref[...] + acc
    if out_scale is not None:
        acc = acc * out_scale
    o_ref[...] = acc.astype(o_ref.dtype)


def _proj(x, xs, w, layer, col0, ncols, out_dtype, *, w_is_nk, residual=None, stack=None,
          out_scale=None, tn=512, tm_target=704):
    m, k = x.shape
    ms = xs.shape[0]
    tm = _pick_tile(m, tm_target, 16)
    assert ncols % tn == 0 and col0 % tn == 0
    cb = col0 // tn
    if w_is_nk:
        w_spec = pl.BlockSpec((None, tn, k), lambda n, i: (layer, cb + n, 0))
        wb_shape = (tn, k)
    else:
        w_spec = pl.BlockSpec((None, k, tn), lambda n, i: (layer, 0, cb + n))
        wb_shape = (k, tn)
    in_specs = [pl.BlockSpec((tm, k), lambda n, i: (i, 0)),
                pl.BlockSpec((ms, k), lambda n, i: (0, 0)), w_spec]
    args = [x, xs, w]
    aliases = {}
    if residual is not None:
        in_specs += [pl.BlockSpec((tm, tn), lambda n, i: (i, n)),
                     pl.BlockSpec((ms, tn), lambda n, i: (0, n))]
        args += list(residual)
        aliases = {3: 0, 4: 1}
    n_prev = 0
    if stack is None:
        out_specs = [pl.BlockSpec((tm, tn), lambda n, i: (i, n)),
                     pl.BlockSpec((ms, tn), lambda n, i: (0, n))]
        out_shape = [jax.ShapeDtypeStruct((m, ncols), out_dtype),
                     jax.ShapeDtypeStruct((ms, ncols), F32)]
    else:
        n_layers, prev, prev_s = stack
        out_specs = [pl.BlockSpec((None, tm, tn), lambda n, i: (layer, i, n)),
                     pl.BlockSpec((None, ms, tn), lambda n, i: (layer, 0, n))]
        out_shape = [jax.ShapeDtypeStruct((n_layers, m, ncols), out_dtype),
                     jax.ShapeDtypeStruct((n_layers, ms, ncols), F32)]
        if prev is not None:
            n_prev = 2
            in_specs += [pl.BlockSpec(memory_space=pl.ANY)] * 2
            aliases = {len(args): 0, len(args) + 1: 1}
            args += [prev, prev_s]
    return pl.pallas_call(
        functools.partial(_proj_kernel, w_is_nk=w_is_nk, has_res=residual is not None,
                          n_prev=n_prev, out_scale=out_scale),
        grid=(ncols // tn, m // tm),
        in_specs=in_specs,
        out_specs=out_specs,
        out_shape=out_shape,
        scratch_shapes=[pltpu.VMEM(wb_shape, BF16)],
        input_output_aliases=aliases,
        compiler_params=_cparams("arbitrary", "arbitrary"),
        name="proj",
    )(*args)


def _tail_kernel(x_ref, w_ref, o_ref):
    o_ref[...] = _dot_nt(x_ref[...], w_ref[...].astype(BF16))


def _tail_proj(x, w_tail):
    m, k = x.shape
    tm = _pick_tile(m, 704, 16)
    return pl.pallas_call(
        _tail_kernel,
        grid=(m // tm,),
        in_specs=[pl.BlockSpec((tm, k), lambda i: (i, 0)),
                  pl.BlockSpec((LANES, k), lambda i: (0, 0))],
        out_specs=pl.BlockSpec((tm, LANES), lambda i: (i, 0)),
        out_shape=jax.ShapeDtypeStruct((m, LANES), F32),
        compiler_params=_cparams("arbitrary"),
        name="tail_proj",
    )(x, w_tail)


def _logf_kernel(z_ref, b_ref, lf_ref, c_ref, carry_ref, *, seg):
    tr = z_ref.shape[0]
    lf = _log_sigmoid(z_ref[...] + b_ref[...])
    lf_ref[...] = lf
    row = _iota2((tr, tr), 0)
    col = _iota2((tr, tr), 1)
    if seg >= tr:
        @pl.when(pl.program_id(1) == 0)
        def _():
            carry_ref[...] = jnp.zeros_like(carry_ref)

        tri = (col <= row).astype(BF16)
        c = _sel_dot(tri, lf) + carry_ref[...]
        c_ref[...] = c
        carry_ref[...] = c[tr - 1:tr, :]
    else:
        tri = ((col <= row) & ((row // seg) == (col // seg))).astype(BF16)
        c_ref[...] = _sel_dot(tri, lf)


def _logf_cumsum(z, bias, nb, t):
    m = nb * t
    if t >= 128:
        tr = _pick_tile(t, 704, 8)
        grid = (nb, t // tr)
        spec = pl.BlockSpec((tr, LANES), lambda b, i: (b * (t // tr) + i, 0))
    else:
        tr = m
        grid = (1, 1)
        spec = pl.BlockSpec((tr, LANES), lambda b, i: (0, 0))
    return pl.pallas_call(
        functools.partial(_logf_kernel, seg=t),
        grid=grid,
        in_specs=[spec, pl.BlockSpec((1, LANES), lambda b, i: (0, 0))],
        out_specs=[spec, spec],
        out_shape=[jax.ShapeDtypeStruct((m, LANES), F32)] * 2,
        scratch_shapes=[pltpu.VMEM((1, LANES), F32)],
        compiler_params=_cparams("arbitrary", "arbitrary"),
        name="logf_cumsum",
    )(z, bias)


def _fox_prompt_kernel(q_ref, k_ref, v_ref, g_ref, cm_ref, cs_ref, y_ref, kb_ref, vb_ref,
                       *, n_meta, tq):
    t_len = q_ref.shape[0]
    nblk = (t_len - n_meta) // tq
    kb_ref[...] = k_ref[...].astype(BF16)
    vb_ref[...] = v_ref[...].astype(BF16)
    km = kb_ref[0:n_meta, :]
    vm = vb_ref[0:n_meta, :]
    cm = cm_ref[...] * LOG2E
    cs = cs_ref[...] * LOG2E

    def emit(r0, n, acc, l):
        o = acc * (1.0 / l)
        y_ref[r0:r0 + n, :] = (o * _silu(g_ref[r0:r0 + n, :])).astype(y_ref.dtype)

    s = _dot_nt(q_ref[0:n_meta, :], km) - cm
    s = jnp.where(_iota2(s.shape, 1) <= _iota2(s.shape, 0), s, -jnp.inf)
    m = jnp.max(s, axis=1, keepdims=True)
    p = jnp.exp2(s - m)
    emit(0, n_meta, _dot(p.astype(BF16), vm), jnp.sum(p, axis=1, keepdims=True))

    for i in range(nblk):
        r0 = n_meta + i * tq
        q = q_ref[r0:r0 + tq, :]
        s0 = _dot_nt(q, km) - cm
        m_i = jnp.max(s0, axis=1, keepdims=True)
        p0 = jnp.exp2(s0 - m_i)
        l_i = jnp.broadcast_to(jnp.sum(p0, axis=1, keepdims=True) * (1.0 / LANES), (tq, LANES))
        acc = _dot(p0.astype(BF16), vm)
        for j in range(i + 1):
            k0 = n_meta + j * tq
            s_ = _dot_nt(q, kb_ref[k0:k0 + tq, :]) - cs[:, j * tq:(j + 1) * tq]
            if j == i:
                s_ = jnp.where(_iota2(s_.shape, 1) <= _iota2(s_.shape, 0), s_, -jnp.inf)
            m_n = jnp.maximum(m_i, jnp.max(s_, axis=1, keepdims=True))
            alpha = jnp.exp2(m_i - m_n)
            p_ = jnp.exp2(s_ - m_n)
            l_i = alpha * l_i + sum(p_[:, c0:c0 + LANES] for c0 in range(0, tq, LANES))
            acc = alpha * acc + _dot(p_.astype(BF16), vb_ref[k0:k0 + tq, :])
            m_i = m_n
        emit(r0, tq, acc, jnp.sum(l_i, axis=1, keepdims=True))


def _fox_prompt(q, k_all, v_all, g, c_meta, c_seq, layer, n_heads):
    b, t, w = q.shape
    dh = w // n_heads
    s_len = t - N_META
    tq = _pick_tile(s_len, 256, LANES)
    blk = lambda: pl.BlockSpec((None, t, dh), lambda bi, hi: (bi, 0, hi))
    lblk = lambda: pl.BlockSpec((None, None, t, dh), lambda bi, hi: (layer, bi, 0, hi))
    return pl.pallas_call(
        functools.partial(_fox_prompt_kernel, n_meta=N_META, tq=tq),
        grid=(b, n_heads),
        in_specs=[blk(), lblk(), lblk(), blk(),
                  pl.BlockSpec((None, None, 1, N_META), lambda bi, hi: (bi, hi, 0, 0)),
                  pl.BlockSpec((None, None, 1, s_len), lambda bi, hi: (bi, hi, 0, 0))],
        out_specs=blk(),
        out_shape=jax.ShapeDtypeStruct((b, t, w), BF16),
        scratch_shapes=[pltpu.VMEM((t, dh), BF16), pltpu.VMEM((t, dh), BF16)],
        compiler_params=_cparams("arbitrary", "arbitrary"),
        name="fox_prompt_attn",
    )(q, k_all, v_all, g, c_meta, c_seq)


SAMPLE_PAGES_PER_STEP = (4, 2, 1)


def _fox_sample_kernel(pt_ref, q_ref, g_ref, kn_ref, vn_ref, cn_ref, *rest, scale, pps):
    del pt_ref
    kp_refs, vp_refs, lfp_refs = rest[:pps], rest[pps:2 * pps], rest[2 * pps:3 * pps]
    y_ref, m_ref, l_ref, acc_ref, carry_ref, wq_ref, lr_ref, neg_ref = rest[3 * pps:]
    p_idx = pl.program_id(1)
    n_steps = pl.num_programs(1)
    page, n_heads, dh = kp_refs[0].shape
    n_new = kn_ref.shape[0]
    n_groups = n_heads // SUBLANES
    grp_rows = SUBLANES * n_new
    grp_keys = SUBLANES * page

    @pl.when(p_idx == 0)
    def _():
        m_ref[...] = jnp.full_like(m_ref, -jnp.inf)
        l_ref[...] = jnp.zeros_like(l_ref)
        acc_ref[...] = jnp.zeros_like(acc_ref)
        carry_ref[...] = jnp.zeros_like(carry_ref)
        wq_ref[...] = (q_ref[...] * (scale * LOG2E)).astype(BF16)
        lr_ref[...] = (_iota2(lr_ref.shape, 0) > _iota2(lr_ref.shape, 1) // SUBLANES).astype(BF16)
        same_head = (_iota2(neg_ref.shape, 1) % SUBLANES) == (_iota2(neg_ref.shape, 0) // n_new)
        neg_ref[...] = jnp.where(same_head, 0.0, -jnp.inf)

    def update(state, s, vb):
        m_i, l_i, acc = state
        m_n = jnp.maximum(m_i, jnp.max(s, axis=1, keepdims=True))
        alpha = jnp.exp2(m_i - m_n)
        p = jnp.exp2(s - m_n)
        return (m_n, alpha * l_i + jnp.sum(p, axis=1, keepdims=True),
                alpha * acc + _dot(p.astype(BF16), vb))

    def load_state():
        m_all, l_all, acc_all = m_ref[...], l_ref[...], acc_ref[...]
        rows = [slice(g * grp_rows, (g + 1) * grp_rows) for g in range(n_groups)]
        return [(m_all[r, :], l_all[r, :], acc_all[r, :]) for r in rows]

    def store_state(states):
        for ref, parts in zip((m_ref, l_ref, acc_ref), zip(*states)):
            ref[...] = parts[0] if n_groups == 1 else jnp.concatenate(parts, axis=0)

    def group(ref, g):
        blk = ref[:, g * SUBLANES:(g + 1) * SUBLANES, :]
        return blk.reshape(blk.shape[0] * SUBLANES, dh).astype(BF16)

    carry = carry_ref[...]
    suffs = []
    for i in range(pps):
        lf = lfp_refs[i][...]
        suffs.append((_dot_sel(lf, lr_ref[...]) + carry) * LOG2E)
        carry = carry + jnp.sum(lf, axis=1, keepdims=True)
    carry_ref[...] = carry
    own_head = (_iota2((SUBLANES, grp_keys), 1) % SUBLANES) == _iota2((SUBLANES, grp_keys), 0)
    states = load_state()
    for g in range(n_groups):
        wq = wq_ref[g * grp_rows:(g + 1) * grp_rows, :]
        s_parts = []
        for i in range(pps):
            sg = suffs[i][g * SUBLANES:(g + 1) * SUBLANES, :]
            bias = jnp.sum(jnp.where(own_head, sg, 0.0), axis=0, keepdims=True)
            s_parts.append(_dot_nt(wq, group(kp_refs[i], g)) + (bias + neg_ref[...]))
        s_past = s_parts[0] if pps == 1 else jnp.concatenate(s_parts, axis=1)
        v_parts = [group(vp_refs[i], g) for i in range(pps)]
        states[g] = update(states[g], s_past, v_parts[0] if pps == 1 else jnp.concatenate(v_parts, axis=0))
    store_state(states)

    @pl.when(p_idx == n_steps - 1)
    def _():
        new_keys = SUBLANES * n_new
        r_ = _iota2((grp_rows, new_keys), 0)
        c_ = _iota2((grp_rows, new_keys), 1)
        valid = ((c_ % SUBLANES) == (r_ // n_new)) & ((c_ // SUBLANES) <= (r_ % n_new))
        final = load_state()
        for g in range(n_groups):
            wq = wq_ref[g * grp_rows:(g + 1) * grp_rows, :]
            s_new = _dot_nt(wq, group(kn_ref, g)) + cn_ref[g] * LOG2E
            final[g] = update(final[g], jnp.where(valid, s_new, -jnp.inf), group(vn_ref, g))
        o = jnp.concatenate([acc * (1.0 / l_f) for _, l_f, acc in final], axis=0)
        y_ref[...] = (o * _silu(g_ref[...])).astype(y_ref.dtype)


def _fox_sample(q, g_new, k_all, v_all, c_new, cache_k, cache_v, cache_lf_t, layer, page_table, n_heads):
    db, s_new, w = q.shape
    n_pages = page_table.shape[1]
    ps = cache_k.shape[2]
    dh = w // n_heads
    n_layers = k_all.shape[0]
    n_groups = n_heads // SUBLANES
    rows = s_new * n_heads
    to_rows = lambda a: a.reshape(db, s_new, n_heads, dh).transpose(0, 2, 1, 3).reshape(db, rows, dh)
    q_r, g_r = to_rows(q), to_rows(g_new)
    kn = k_all.reshape(n_layers, db, s_new, n_heads, dh)
    vn = v_all.reshape(n_layers, db, s_new, n_heads, dh)
    cn = -c_new.reshape(db, s_new, n_groups, SUBLANES).transpose(0, 2, 1, 3)
    cn = cn.reshape(db, n_groups, 1, s_new * SUBLANES)
    per_b = lambda: pl.BlockSpec((None, rows, dh), lambda b, p, pt: (b, 0, 0))
    new_spec = lambda: pl.BlockSpec((None, None, s_new, n_heads, dh), lambda b, p, pt: (layer, b, 0, 0, 0))
    pps = max(c for c in SAMPLE_PAGES_PER_STEP if n_pages % c == 0)

    def page_spec(i):
        return pl.BlockSpec(
            (None, None, ps, n_heads, dh),
            lambda b, p, pt: (layer, pt[b, n_pages - 1 - (p * pps + i)], 0, 0, 0))

    def lf_spec(i):
        return pl.BlockSpec(
            (None, None, n_heads, ps),
            lambda b, p, pt: (layer, pt[b, n_pages - 1 - (p * pps + i)], 0, 0))

    grid_spec = pltpu.PrefetchScalarGridSpec(
        num_scalar_prefetch=1,
        grid=(db, n_pages // pps),
        in_specs=[per_b(), per_b(), new_spec(), new_spec(),
                  pl.BlockSpec((None, n_groups, 1, s_new * SUBLANES), lambda b, p, pt: (b, 0, 0, 0))]
                 + [page_spec(i) for i in range(pps)] + [page_spec(i) for i in range(pps)]
                 + [lf_spec(i) for i in range(pps)],
        out_specs=per_b(),
        scratch_shapes=[pltpu.VMEM((rows, 1), F32), pltpu.VMEM((rows, 1), F32),
                        pltpu.VMEM((rows, dh), F32), pltpu.VMEM((n_heads, 1), F32),
                        pltpu.VMEM((rows, dh), BF16), pltpu.VMEM((ps, ps * SUBLANES), BF16),
                        pltpu.VMEM((SUBLANES * s_new, ps * SUBLANES), F32)],
    )
    y = pl.pallas_call(
        functools.partial(_fox_sample_kernel, scale=dh ** -0.5, pps=pps),
        grid_spec=grid_spec,
        out_shape=jax.ShapeDtypeStruct((db, rows, dh), BF16),
        compiler_params=_cparams("arbitrary", "arbitrary"),
        name="fox_sample_attn",
    )(page_table, q_r, g_r, kn, vn, cn, *([cache_k] * pps), *([cache_v] * pps), *([cache_lf_t] * pps))
    return y.reshape(db, n_heads, s_new, dh).transpose(0, 2, 1, 3).reshape(db, s_new, w)


GLA_CHUNK = 256
GLA_UNROLL = 2


def _gla_chunk(r0, c, refs, wa, q_scale):
    q_ref, k_ref, v_ref, g_ref, a_ref, bg_ref, on_ref, y_ref, s_ref = refs
    rows = pl.ds(r0, c)
    q = q_ref[rows, :] * q_scale
    k = k_ref[rows, :]
    v = v_ref[rows, :].astype(BF16)
    dk = q.shape[1]
    x = _dot(a_ref[rows, :].astype(BF16), wa) + bg_ref[...]
    la = _log_sigmoid(x) * (LOG2E / GLA_GATE_TAU)
    row = _iota2((c, c), 0)
    col = _iota2((c, c), 1)
    b = _sel_dot((col <= row).astype(BF16), la)
    b_last = b[c - 1:c, :]

    s_old = s_ref[...]
    o = _dot((q * jnp.exp2(b)).astype(BF16), s_old.astype(BF16))

    a_mat = jnp.where(row == col, jnp.sum(q * k, axis=1, keepdims=True), 0.0)
    xor = row ^ col
    sub = _iota2((c, dk), 0) % SUBLANES

    def tile_rows(first, stride, n):
        pieces = [jnp.broadcast_to(b[r:r + 1, :], (n, dk)) for r in range(first, c, stride)]
        return pieces[0] if len(pieces) == 1 else jnp.concatenate(pieces, axis=0)

    blk = 2
    while blk <= c:
        half = blk // 2
        if blk < SUBLANES:
            b_mid = tile_rows(half - 1, SUBLANES, SUBLANES)
            for s0 in range(blk, SUBLANES, blk):
                b_mid = jnp.where(sub >= s0, tile_rows(s0 + half - 1, SUBLANES, SUBLANES), b_mid)
        else:
            b_mid = tile_rows(half - 1, blk, blk)
        q_t = (q * jnp.exp2(b - b_mid)).astype(BF16)
        k_t = (k * jnp.exp2(b_mid - b)).astype(BF16)
        a_mat = jnp.where(xor >= half, _dot_nt(q_t, k_t), a_mat)
        blk *= 2
    a_mat = jnp.where(col <= row, a_mat, 0.0)
    o = o + _dot(a_mat.astype(BF16), v)

    k_end = (k * jnp.exp2(b_last - b)).astype(BF16)
    decay = jnp.transpose(jnp.broadcast_to(jnp.exp2(b_last), (8, dk)))[:, 0:1]
    s_ref[...] = decay * s_old + _dot_tn(k_end, v)

    o = o * lax.rsqrt(jnp.mean(o * o, axis=-1, keepdims=True) + EPS) * on_ref[...]
    y_ref[rows, :] = (o * _silu(g_ref[rows, :])).astype(y_ref.dtype)


def _gla_kernel(*refs, head, n_chunks, has_state, has_prev, q_scale):
    q_ref, k_ref, v_ref, g_ref, a_ref, wa_ref, bg_ref, on_ref = refs[:8]
    y_ref, so_ref, s_ref = refs[8 + has_state + has_prev:]
    if has_state:
        s_ref[...] = refs[8][...]
    else:
        s_ref[...] = jnp.zeros_like(s_ref)
    wa = wa_ref[...].astype(BF16)
    crefs = (q_ref, k_ref, v_ref, g_ref, a_ref, bg_ref, on_ref, y_ref, s_ref)
    if head:
        _gla_chunk(0, head, crefs, wa, q_scale)

    def body(i, carry):
        _gla_chunk(pl.multiple_of(head + i * GLA_CHUNK, 16), GLA_CHUNK, crefs, wa, q_scale)
        return carry

    if n_chunks:
        lax.fori_loop(0, n_chunks, body, 0, unroll=GLA_UNROLL if n_chunks % GLA_UNROLL == 0 else 1)
    so_ref[...] = s_ref[...]


def _gla(zq, zk, zv, zg, za, wa2, b_gate, onorm, state0, layer, n_layers, s_prev, n_heads):
    b, t, kw = zq.shape
    vw = zv.shape[2]
    dk, dv = kw // n_heads, vw // n_heads
    head = t % GLA_CHUNK
    n_chunks = t // GLA_CHUNK
    assert head in (0, 8, 16, 32, 64)
    tok = lambda last: pl.BlockSpec((None, t, last), lambda bi, hi: (bi, 0, hi))
    in_specs = [tok(dk), tok(dk), tok(dv), tok(dv),
                pl.BlockSpec((None, t, LANES), lambda bi, hi: (bi, 0, 0)),
                pl.BlockSpec((LANES, dk), lambda bi, hi: (0, hi)),
                pl.BlockSpec((1, dk), lambda bi, hi: (0, hi)),
                pl.BlockSpec((1, dv), lambda bi, hi: (0, 0))]
    args = [zq, zk, zv, zg, za, wa2, b_gate, onorm]
    if state0 is not None:
        in_specs.append(pl.BlockSpec((None, None, None, dk, dv), lambda bi, hi: (layer, bi, hi, 0, 0)))
        args.append(state0)
    aliases = {}
    if s_prev is not None:
        in_specs.append(pl.BlockSpec(memory_space=pl.ANY))
        aliases = {len(args): 1}
        args.append(s_prev)
    return pl.pallas_call(
        functools.partial(_gla_kernel, head=head, n_chunks=n_chunks, has_state=state0 is not None,
                          has_prev=s_prev is not None, q_scale=dk ** -0.5),
        grid=(b, n_heads),
        in_specs=in_specs,
        out_specs=[tok(dv),
                   pl.BlockSpec((None, None, None, dk, dv), lambda bi, hi: (layer, bi, hi, 0, 0))],
        out_shape=[jax.ShapeDtypeStruct((b, t, vw), BF16),
                   jax.ShapeDtypeStruct((n_layers, b, n_heads, dk, dv), F32)],
        scratch_shapes=[pltpu.VMEM((dk, dv), F32)],
        input_output_aliases=aliases,
        compiler_params=_cparams("arbitrary", "arbitrary"),
        name="gla",
    )(*args)


def _pad_rows(x):
    return jnp.pad(x, ((0, LANES - x.shape[0]), (0, 0)))


def _fox_layer(h_p, h_s, dims, norm, w_in_t, b_f, w_out, layer, n_heads, prev, sample_ctx):
    bsz, t, db, ds = dims
    d = h_p.shape[1]
    aw = w_out.shape[1]
    n_layers = w_out.shape[0]
    dh = aw // n_heads
    kp_prev, vp_prev, ks_prev, vs_prev = prev
    u_p = _rmsnorm(h_p, norm[layer], BF16)
    u_s = _rmsnorm(h_s, norm[layer], BF16)
    proj_in = functools.partial(_proj, u_p, u_s, w_in_t, layer, w_is_nk=True)
    q_p, q_s = proj_in(0, aw, BF16, out_scale=dh ** -0.5 * LOG2E)
    k_p, k_s = proj_in(aw, aw, F32, stack=(n_layers, kp_prev, ks_prev))
    v_p, v_s = proj_in(2 * aw, aw, F32, stack=(n_layers, vp_prev, vs_prev))
    g_p, g_s = proj_in(3 * aw, aw, F32)
    w_tail = _pad_rows(w_in_t[layer, 4 * aw:, :])
    bias = jnp.pad(b_f[layer][None, :].astype(F32), ((0, 0), (0, LANES - n_heads)))
    lf_p, c_p = _logf_cumsum(_tail_proj(u_p, w_tail), bias, bsz, t)
    lf_s, c_s = _logf_cumsum(_tail_proj(u_s, w_tail), bias, db, ds)
    lf_p = lf_p[:, :n_heads].reshape(bsz, t, n_heads)
    lf_s = lf_s[:, :n_heads].reshape(db, ds, n_heads)
    c_t = jnp.transpose(c_p[:, :n_heads].reshape(bsz, t, n_heads), (0, 2, 1))[:, :, None, :]
    y_p = _fox_prompt(q_p.reshape(bsz, t, aw), k_p.reshape(n_layers, bsz, t, aw),
                      v_p.reshape(n_layers, bsz, t, aw), g_p.reshape(bsz, t, aw),
                      c_t[..., :N_META], c_t[..., N_META:], layer, n_heads)
    cache_k, cache_v, cache_lf_t, page_table = sample_ctx
    y_s = _fox_sample(q_s.reshape(db, ds, aw), g_s.reshape(db, ds, aw), k_s, v_s,
                      c_s[:, :n_heads].reshape(db, ds, n_heads),
                      cache_k, cache_v, cache_lf_t, layer, page_table, n_heads)
    h_p, h_s = _proj(y_p.reshape(bsz * t, aw), y_s.reshape(db * ds, aw), w_out, layer, 0, d, F32,
                     w_is_nk=False, residual=(h_p, h_s))
    return h_p, h_s, (k_p, v_p, k_s, v_s), lf_p, lf_s


def _gla_layer(h_p, h_s, dims, norm, w_in_t, w_a2, b_gate, onorm, w_out, layer, n_heads, state0, prev):
    bsz, t, db, ds = dims
    d = h_p.shape[1]
    vw = w_out.shape[1]
    kw = w_a2.shape[2]
    n_layers = w_out.shape[0]
    u_p = _rmsnorm(h_p, norm[layer], BF16)
    u_s = _rmsnorm(h_s, norm[layer], BF16)
    proj_in = functools.partial(_proj, u_p, u_s, w_in_t, layer, w_is_nk=True)
    zq_p, zq_s = proj_in(0, kw, F32)
    zk_p, zk_s = proj_in(kw, kw, F32)
    zv_p, zv_s = proj_in(2 * kw, vw, F32)
    zg_p, zg_s = proj_in(2 * kw + vw, vw, F32)
    w_tail = _pad_rows(w_in_t[layer, 2 * kw + 2 * vw:, :])
    za_p, za_s = _tail_proj(u_p, w_tail), _tail_proj(u_s, w_tail)
    wa2 = _pad_rows(w_a2[layer])
    bg, on = b_gate[layer][None, :], onorm[layer][None, :]
    y_p, st_p = _gla(zq_p.reshape(bsz, t, kw), zk_p.reshape(bsz, t, kw), zv_p.reshape(bsz, t, vw),
                     zg_p.reshape(bsz, t, vw), za_p.reshape(bsz, t, LANES), wa2, bg, on,
                     None, layer, n_layers, prev[0], n_heads)
    y_s, st_s = _gla(zq_s.reshape(db, ds, kw), zk_s.reshape(db, ds, kw), zv_s.reshape(db, ds, vw),
                     zg_s.reshape(db, ds, vw), za_s.reshape(db, ds, LANES), wa2, bg, on,
                     state0, layer, n_layers, prev[1], n_heads)
    h_p, h_s = _proj(y_p.reshape(bsz * t, vw), y_s.reshape(db * ds, vw), w_out, layer, 0, d, F32,
                     w_is_nk=False, residual=(h_p, h_s))
    return h_p, h_s, (st_p, st_s)


def kernel(x_prompt, x_sample, cache_k, cache_v, cache_logf, state_gla, page_table, meta_tokens,
           norm_a, w_in_a, b_forget, w_out_a, norm_b, w_in_b, w_gate_up, b_gate, onorm_b, w_out_b,
           final_norm):
    bsz, seq, d = x_prompt.shape
    db, ds, _ = x_sample.shape
    a_heads = b_forget.shape[1]
    b_heads = state_gla.shape[2]
    dh = w_out_a.shape[1] // a_heads
    depth = norm_a.shape[0] + norm_b.shape[0]
    t = N_META + seq
    meta = jnp.broadcast_to(meta_tokens.astype(x_prompt.dtype)[None], (bsz, N_META, d))
    h_p = jnp.concatenate([meta, x_prompt], axis=1).reshape(bsz * t, d)
    h_s = x_sample.reshape(db * ds, d)
    w_in_a_t = jnp.swapaxes(w_in_a, 1, 2)
    w_in_b_t = jnp.swapaxes(w_in_b, 1, 2)
    cache_lf_t = jnp.swapaxes(cache_logf, 2, 3)
    dims = (bsz, t, db, ds)
    kv = (None, None, None, None)
    states = (None, None)
    nf_p, nf_s = [], []
    for i in range(depth):
        j = i // 2
        if i % 2 == 0:
            h_p, h_s, kv, lf_p, lf_s = _fox_layer(
                h_p, h_s, dims, norm_a, w_in_a_t, b_forget, w_out_a, j, a_heads, kv,
                (cache_k, cache_v, cache_lf_t, page_table))
            nf_p.append(lf_p)
            nf_s.append(lf_s)
        else:
            h_p, h_s, states = _gla_layer(
                h_p, h_s, dims, norm_b, w_in_b_t, w_gate_up, b_gate, onorm_b, w_out_b, j, b_heads,
                state_gla, states)
    y_prompt = _rmsnorm(h_p, final_norm, F32).reshape(bsz, t, d)[:, N_META:]
    y_sample = _rmsnorm(h_s, final_norm, F32).reshape(db, ds, d)
    n_a = norm_a.shape[0]
    kp_shape = (n_a, bsz, t, a_heads, dh)
    ks_shape = (n_a, db, ds, a_heads, dh)
    return (y_prompt, y_sample, kv[0].reshape(kp_shape), kv[1].reshape(kp_shape), jnp.stack(nf_p),
            kv[2].reshape(ks_shape), kv[3].reshape(ks_shape), jnp.stack(nf_s), states[0], states[1])
```

```python
import functools

import jax
import jax.numpy as jnp
from jax import lax
from jax.experimental import pallas as pl
from jax.experimental.pallas import tpu as pltpu

EPS = 1e-6
N_META = 16
GLA_GATE_TAU = 16.0
LOG2E = 1.4426950408889634
LANES = 128
SUBLANES = 8
VMEM_LIMIT_BYTES = 56 * 1024 * 1024
F32 = jnp.float32
BF16 = jnp.bfloat16


def _cparams(*sem):
    return pltpu.CompilerParams(dimension_semantics=sem, vmem_limit_bytes=VMEM_LIMIT_BYTES)


def _pick_tile(n, target, mult):
    best = None
    for t in range(mult, min(n, target) + 1, mult):
        if n % t == 0:
            best = t
    return best if best is not None else n


def _dot(a, b):
    return jnp.dot(a, b, preferred_element_type=F32)


def _dot_nt(a, b):
    return lax.dot_general(a, b, (((1,), (1,)), ((), ())), preferred_element_type=F32)


def _dot_tn(a, b):
    return lax.dot_general(a, b, (((0,), (0,)), ((), ())), preferred_element_type=F32)


def _split3(x):
    hi = x.astype(BF16)
    r1 = x - hi.astype(F32)
    mid = r1.astype(BF16)
    lo = (r1 - mid.astype(F32)).astype(BF16)
    return hi, mid, lo


def _sel_dot(sel, x):
    return sum(_dot(sel, part) for part in _split3(x))


def _dot_sel(x, sel):
    return sum(_dot(part, sel) for part in _split3(x))


def _log_sigmoid(x):
    return jnp.minimum(x, 0.0) - jnp.log1p(jnp.exp(-jnp.abs(x)))


def _silu(x):
    return x * (1.0 / (1.0 + jnp.exp(-x)))


def _iota2(shape, dim):
    return lax.broadcasted_iota(jnp.int32, shape, dim)


def _rmsnorm_kernel(x_ref, g_ref, o_ref):
    x = x_ref[...]
    ms = jnp.mean(x * x, axis=-1, keepdims=True)
    o_ref[...] = (x * lax.rsqrt(ms + EPS) * g_ref[...]).astype(o_ref.dtype)


def _rmsnorm(x, g, out_dtype):
    m, d = x.shape
    tm = _pick_tile(m, 512, 16)
    return pl.pallas_call(
        _rmsnorm_kernel,
        grid=(m // tm,),
        in_specs=[pl.BlockSpec((tm, d), lambda i: (i, 0)),
                  pl.BlockSpec((1, d), lambda i: (0, 0))],
        out_specs=pl.BlockSpec((tm, d), lambda i: (i, 0)),
        out_shape=jax.ShapeDtypeStruct((m, d), out_dtype),
        compiler_params=_cparams("arbitrary"),
        name="rmsnorm",
    )(x, g.reshape(1, d))


def _proj_kernel(*refs, w_is_nk, has_res, n_prev, out_scale):
    x_ref, xs_ref, w_ref = refs[:3]
    r_ref, rs_ref = refs[3:5] if has_res else (None, None)
    o_ref, os_ref, wb_ref = refs[(5 if has_res else 3) + n_prev:]

    def mm(x):
        return _dot_nt(x, wb_ref[...]) if w_is_nk else _dot(x, wb_ref[...])

    @pl.when(pl.program_id(1) == 0)
    def _():
        wb_ref[...] = w_ref[...].astype(BF16)
        acc_s = mm(xs_ref[...])
        if has_res:
            acc_s = rs_ref[...] + acc_s
        os_ref[...] = acc_s.astype(os_ref.dtype)

    acc = mm(x_ref[...])
    if has_res:
        acc = r_ref[...] + acc
    if out_scale is not None:
        acc = acc * out_scale
    o_ref[...] = acc.astype(o_ref.dtype)


def _proj(x, xs, w, layer, col0, ncols, out_dtype, *, w_is_nk, residual=None, stack=None,
          out_scale=None, tn=512, tm_target=704):
    m, k = x.shape
    ms = xs.shape[0]
    tm = _pick_tile(m, tm_target, 16)
    assert ncols % tn == 0 and col0 % tn == 0
    cb = col0 // tn
    if w_is_nk:
        w_spec = pl.BlockSpec((None, tn, k), lambda n, i: (layer, cb + n, 0))
        wb_shape = (tn, k)
    else:
        w_spec = pl.BlockSpec((None, k, tn), lambda n, i: (layer, 0, cb + n))
        wb_shape = (k, tn)
    in_specs = [pl.BlockSpec((tm, k), lambda n, i: (i, 0)),
                pl.BlockSpec((ms, k), lambda n, i: (0, 0)), w_spec]
    args = [x, xs, w]
    aliases = {}
    if residual is not None:
        in_specs += [pl.BlockSpec((tm, tn), lambda n, i: (i, n)),
                     pl.BlockSpec((ms, tn), lambda n, i: (0, n))]
        args += list(residual)
        aliases = {3: 0, 4: 1}
    n_prev = 0
    if stack is None:
        out_specs = [pl.BlockSpec((tm, tn), lambda n, i: (i, n)),
                     pl.BlockSpec((ms, tn), lambda n, i: (0, n))]
        out_shape = [jax.ShapeDtypeStruct((m, ncols), out_dtype),
                     jax.ShapeDtypeStruct((ms, ncols), F32)]
    else:
        n_layers, prev, prev_s = stack
        out_specs = [pl.BlockSpec((None, tm, tn), lambda n, i: (layer, i, n)),
                     pl.BlockSpec((None, ms, tn), lambda n, i: (layer, 0, n))]
        out_shape = [jax.ShapeDtypeStruct((n_layers, m, ncols), out_dtype),
                     jax.ShapeDtypeStruct((n_layers, ms, ncols), F32)]
        if prev is not None:
            n_prev = 2
            in_specs += [pl.BlockSpec(memory_space=pl.ANY)] * 2
            aliases = {len(args): 0, len(args) + 1: 1}
            args += [prev, prev_s]
    return pl.pallas_call(
        functools.partial(_proj_kernel, w_is_nk=w_is_nk, has_res=residual is not None,
                          n_prev=n_prev, out_scale=out_scale),
        grid=(ncols // tn, m // tm),
        in_specs=in_specs,
        out_specs=out_specs,
        out_shape=out_shape,
        scratch_shapes=[pltpu.VMEM(wb_shape, BF16)],
        input_output_aliases=aliases,
        compiler_params=_cparams("arbitrary", "arbitrary"),
        name="proj",
    )(*args)


def _rmsnorm_tail_kernel(x_ref, g_ref, w_ref, u_ref, z_ref):
    _rmsnorm_kernel(x_ref, g_ref, u_ref)
    z_ref[...] = _dot_nt(u_ref[...], w_ref[...].astype(BF16))


def _rmsnorm_tail(x, g, w_tail):
    m, d = x.shape
    tm = _pick_tile(m, 512, 16)
    return pl.pallas_call(
        _rmsnorm_tail_kernel,
        grid=(m // tm,),
        in_specs=[pl.BlockSpec((tm, d), lambda i: (i, 0)),
                  pl.BlockSpec((1, d), lambda i: (0, 0)),
                  pl.BlockSpec((LANES, d), lambda i: (0, 0))],
        out_specs=[pl.BlockSpec((tm, d), lambda i: (i, 0)),
                   pl.BlockSpec((tm, LANES), lambda i: (i, 0))],
        out_shape=[jax.ShapeDtypeStruct((m, d), BF16), jax.ShapeDtypeStruct((m, LANES), F32)],
        compiler_params=_cparams("arbitrary"),
        name="rmsnorm_tail",
    )(x, g.reshape(1, d), w_tail)


def _logf_kernel(z_ref, b_ref, lf_ref, c_ref, carry_ref, *, seg):
    tr = z_ref.shape[0]
    lf = _log_sigmoid(z_ref[...] + b_ref[...])
    lf_ref[...] = lf
    row = _iota2((tr, tr), 0)
    col = _iota2((tr, tr), 1)
    if seg >= tr:
        @pl.when(pl.program_id(1) == 0)
        def _():
            carry_ref[...] = jnp.zeros_like(carry_ref)

        tri = (col <= row).astype(BF16)
        c = _sel_dot(tri, lf) + carry_ref[...]
        c_ref[...] = c
        carry_ref[...] = c[tr - 1:tr, :]
    else:
        tri = ((col <= row) & ((row // seg) == (col // seg))).astype(BF16)
        c_ref[...] = _sel_dot(tri, lf)


def _logf_cumsum(z, bias, nb, t):
    m = nb * t
    if t >= 128:
        tr = _pick_tile(t, 704, 8)
        grid = (nb, t // tr)
        spec = pl.BlockSpec((tr, LANES), lambda b, i: (b * (t // tr) + i, 0))
    else:
        tr = m
        grid = (1, 1)
        spec = pl.BlockSpec((tr, LANES), lambda b, i: (0, 0))
    return pl.pallas_call(
        functools.partial(_logf_kernel, seg=t),
        grid=grid,
        in_specs=[spec, pl.BlockSpec((1, LANES), lambda b, i: (0, 0))],
        out_specs=[spec, spec],
        out_shape=[jax.ShapeDtypeStruct((m, LANES), F32)] * 2,
        scratch_shapes=[pltpu.VMEM((1, LANES), F32)],
        compiler_params=_cparams("arbitrary", "arbitrary"),
        name="logf_cumsum",
    )(z, bias)


def _fox_prompt_kernel(q_ref, k_ref, v_ref, g_ref, cm_ref, cs_ref, y_ref, kb_ref, vb_ref,
                       *, n_meta, tq):
    t_len = q_ref.shape[0]
    nblk = (t_len - n_meta) // tq
    kb_ref[...] = k_ref[...].astype(BF16)
    vb_ref[...] = v_ref[...].astype(BF16)
    km = kb_ref[0:n_meta, :]
    vm = vb_ref[0:n_meta, :]
    cm = cm_ref[...] * LOG2E
    cs = cs_ref[...] * LOG2E

    def emit(r0, n, acc, l):
        o = acc * (1.0 / l)
        y_ref[r0:r0 + n, :] = (o * _silu(g_ref[r0:r0 + n, :])).astype(y_ref.dtype)

    s = _dot_nt(q_ref[0:n_meta, :], km) - cm
    s = jnp.where(_iota2(s.shape, 1) <= _iota2(s.shape, 0), s, -jnp.inf)
    m = jnp.max(s, axis=1, keepdims=True)
    p = jnp.exp2(s - m)
    emit(0, n_meta, _dot(p.astype(BF16), vm), jnp.sum(p, axis=1, keepdims=True))

    for i in range(nblk):
        r0 = n_meta + i * tq
        q = q_ref[r0:r0 + tq, :]
        s0 = _dot_nt(q, km) - cm
        m_i = jnp.max(s0, axis=1, keepdims=True)
        p0 = jnp.exp2(s0 - m_i)
        l_i = jnp.broadcast_to(jnp.sum(p0, axis=1, keepdims=True) * (1.0 / LANES), (tq, LANES))
        acc = _dot(p0.astype(BF16), vm)
        for j in range(i + 1):
            k0 = n_meta + j * tq
            s_ = _dot_nt(q, kb_ref[k0:k0 + tq, :]) - cs[:, j * tq:(j + 1) * tq]
            if j == i:
                s_ = jnp.where(_iota2(s_.shape, 1) <= _iota2(s_.shape, 0), s_, -jnp.inf)
            m_n = jnp.maximum(m_i, jnp.max(s_, axis=1, keepdims=True))
            alpha = jnp.exp2(m_i - m_n)
            p_ = jnp.exp2(s_ - m_n)
            l_i = alpha * l_i + sum(p_[:, c0:c0 + LANES] for c0 in range(0, tq, LANES))
            acc = alpha * acc + _dot(p_.astype(BF16), vb_ref[k0:k0 + tq, :])
            m_i = m_n
        emit(r0, tq, acc, jnp.sum(l_i, axis=1, keepdims=True))


def _fox_prompt(q, k_all, v_all, g, c_meta, c_seq, layer, n_heads):
    b, t, w = q.shape
    dh = w // n_heads
    s_len = t - N_META
    tq = _pick_tile(s_len, 256, LANES)
    blk = lambda: pl.BlockSpec((None, t, dh), lambda bi, hi: (bi, 0, hi))
    lblk = lambda: pl.BlockSpec((None, None, t, dh), lambda bi, hi: (layer, bi, 0, hi))
    return pl.pallas_call(
        functools.partial(_fox_prompt_kernel, n_meta=N_META, tq=tq),
        grid=(b, n_heads),
        in_specs=[blk(), lblk(), lblk(), blk(),
                  pl.BlockSpec((None, None, 1, N_META), lambda bi, hi: (bi, hi, 0, 0)),
                  pl.BlockSpec((None, None, 1, s_len), lambda bi, hi: (bi, hi, 0, 0))],
        out_specs=blk(),
        out_shape=jax.ShapeDtypeStruct((b, t, w), BF16),
        scratch_shapes=[pltpu.VMEM((t, dh), BF16), pltpu.VMEM((t, dh), BF16)],
        compiler_params=_cparams("arbitrary", "arbitrary"),
        name="fox_prompt_attn",
    )(q, k_all, v_all, g, c_meta, c_seq)


SAMPLE_PAGES_PER_STEP = (4, 2, 1)


def _fox_sample_kernel(pt_ref, q_ref, g_ref, kn_ref, vn_ref, cn_ref, *rest, scale, pps):
    del pt_ref
    kp_refs, vp_refs, lfp_refs = rest[:pps], rest[pps:2 * pps], rest[2 * pps:3 * pps]
    y_ref, m_ref, l_ref, acc_ref, carry_ref, wq_ref, lr_ref, neg_ref = rest[3 * pps:]
    p_idx = pl.program_id(1)
    n_steps = pl.num_programs(1)
    page, n_heads, dh = kp_refs[0].shape
    n_new = kn_ref.shape[0]
    n_groups = n_heads // SUBLANES
    grp_rows = SUBLANES * n_new
    grp_keys = SUBLANES * page

    @pl.when(p_idx == 0)
    def _():
        m_ref[...] = jnp.full_like(m_ref, -jnp.inf)
        l_ref[...] = jnp.zeros_like(l_ref)
        acc_ref[...] = jnp.zeros_like(acc_ref)
        carry_ref[...] = jnp.zeros_like(carry_ref)
        wq_ref[...] = (q_ref[...] * (scale * LOG2E)).astype(BF16)
        lr_ref[...] = (_iota2(lr_ref.shape, 0) > _iota2(lr_ref.shape, 1) // SUBLANES).astype(BF16)
        same_head = (_iota2(neg_ref.shape, 1) % SUBLANES) == (_iota2(neg_ref.shape, 0) // n_new)
        neg_ref[...] = jnp.where(same_head, 0.0, -jnp.inf)

    def update(state, s, vb):
        m_i, l_i, acc = state
        m_n = jnp.maximum(m_i, jnp.max(s, axis=1, keepdims=True))
        alpha = jnp.exp2(m_i - m_n)
        p = jnp.exp2(s - m_n)
        return (m_n, alpha * l_i + jnp.sum(p, axis=1, keepdims=True),
                alpha * acc + _dot(p.astype(BF16), vb))

    def load_state():
        m_all, l_all, acc_all = m_ref[...], l_ref[...], acc_ref[...]
        rows = [slice(g * grp_rows, (g + 1) * grp_rows) for g in range(n_groups)]
        return [(m_all[r, :], l_all[r, :], acc_all[r, :]) for r in rows]

    def store_state(states):
        for ref, parts in zip((m_ref, l_ref, acc_ref), zip(*states)):
            ref[...] = parts[0] if n_groups == 1 else jnp.concatenate(parts, axis=0)

    def group(ref, g):
        blk = ref[:, g * SUBLANES:(g + 1) * SUBLANES, :]
        return blk.reshape(blk.shape[0] * SUBLANES, dh).astype(BF16)

    carry = carry_ref[...]
    suffs = []
    for i in range(pps):
        lf = lfp_refs[i][...]
        suffs.append((_dot_sel(lf, lr_ref[...]) + carry) * LOG2E)
        carry = carry + jnp.sum(lf, axis=1, keepdims=True)
    carry_ref[...] = carry
    own_head = (_iota2((SUBLANES, grp_keys), 1) % SUBLANES) == _iota2((SUBLANES, grp_keys), 0)
    states = load_state()
    for g in range(n_groups):
        wq = wq_ref[g * grp_rows:(g + 1) * grp_rows, :]
        s_parts = []
        for i in range(pps):
            sg = suffs[i][g * SUBLANES:(g + 1) * SUBLANES, :]
            bias = jnp.sum(jnp.where(own_head, sg, 0.0), axis=0, keepdims=True)
            s_parts.append(_dot_nt(wq, group(kp_refs[i], g)) + (bias + neg_ref[...]))
        s_past = s_parts[0] if pps == 1 else jnp.concatenate(s_parts, axis=1)
        v_parts = [group(vp_refs[i], g) for i in range(pps)]
        states[g] = update(states[g], s_past, v_parts[0] if pps == 1 else jnp.concatenate(v_parts, axis=0))
    store_state(states)

    @pl.when(p_idx == n_steps - 1)
    def _():
        new_keys = SUBLANES * n_new
        r_ = _iota2((grp_rows, new_keys), 0)
        c_ = _iota2((grp_rows, new_keys), 1)
        valid = ((c_ % SUBLANES) == (r_ // n_new)) & ((c_ // SUBLANES) <= (r_ % n_new))
        final = load_state()
        for g in range(n_groups):
            wq = wq_ref[g * grp_rows:(g + 1) * grp_rows, :]
            s_new = _dot_nt(wq, group(kn_ref, g)) + cn_ref[g] * LOG2E
            final[g] = update(final[g], jnp.where(valid, s_new, -jnp.inf), group(vn_ref, g))
        o = jnp.concatenate([acc * (1.0 / l_f) for _, l_f, acc in final], axis=0)
        y_ref[...] = (o * _silu(g_ref[...])).astype(y_ref.dtype)


def _fox_sample(q, g_new, k_all, v_all, c_new, cache_k, cache_v, cache_lf_t, layer, page_table, n_heads):
    db, s_new, w = q.shape
    n_pages = page_table.shape[1]
    ps = cache_k.shape[2]
    dh = w // n_heads
    n_layers = k_all.shape[0]
    n_groups = n_heads // SUBLANES
    rows = s_new * n_heads
    to_rows = lambda a: a.reshape(db, s_new, n_heads, dh).transpose(0, 2, 1, 3).reshape(db, rows, dh)
    q_r, g_r = to_rows(q), to_rows(g_new)
    kn = k_all.reshape(n_layers, db, s_new, n_heads, dh)
    vn = v_all.reshape(n_layers, db, s_new, n_heads, dh)
    cn = -c_new.reshape(db, s_new, n_groups, SUBLANES).transpose(0, 2, 1, 3)
    cn = cn.reshape(db, n_groups, 1, s_new * SUBLANES)
    per_b = lambda: pl.BlockSpec((None, rows, dh), lambda b, p, pt: (b, 0, 0))
    new_spec = lambda: pl.BlockSpec((None, None, s_new, n_heads, dh), lambda b, p, pt: (layer, b, 0, 0, 0))
    pps = max(c for c in SAMPLE_PAGES_PER_STEP if n_pages % c == 0)

    def page_spec(i):
        return pl.BlockSpec(
            (None, None, ps, n_heads, dh),
            lambda b, p, pt: (layer, pt[b, n_pages - 1 - (p * pps + i)], 0, 0, 0))

    def lf_spec(i):
        return pl.BlockSpec(
            (None, None, n_heads, ps),
            lambda b, p, pt: (layer, pt[b, n_pages - 1 - (p * pps + i)], 0, 0))

    grid_spec = pltpu.PrefetchScalarGridSpec(
        num_scalar_prefetch=1,
        grid=(db, n_pages // pps),
        in_specs=[per_b(), per_b(), new_spec(), new_spec(),
                  pl.BlockSpec((None, n_groups, 1, s_new * SUBLANES), lambda b, p, pt: (b, 0, 0, 0))]
                 + [page_spec(i) for i in range(pps)] + [page_spec(i) for i in range(pps)]
                 + [lf_spec(i) for i in range(pps)],
        out_specs=per_b(),
        scratch_shapes=[pltpu.VMEM((rows, 1), F32), pltpu.VMEM((rows, 1), F32),
                        pltpu.VMEM((rows, dh), F32), pltpu.VMEM((n_heads, 1), F32),
                        pltpu.VMEM((rows, dh), BF16), pltpu.VMEM((ps, ps * SUBLANES), BF16),
                        pltpu.VMEM((SUBLANES * s_new, ps * SUBLANES), F32)],
    )
    y = pl.pallas_call(
        functools.partial(_fox_sample_kernel, scale=dh ** -0.5, pps=pps),
        grid_spec=grid_spec,
        out_shape=jax.ShapeDtypeStruct((db, rows, dh), BF16),
        compiler_params=_cparams("arbitrary", "arbitrary"),
        name="fox_sample_attn",
    )(page_table, q_r, g_r, kn, vn, cn, *([cache_k] * pps), *([cache_v] * pps), *([cache_lf_t] * pps))
    return y.reshape(db, n_heads, s_new, dh).transpose(0, 2, 1, 3).reshape(db, s_new, w)


GLA_CHUNK = 256
GLA_UNROLL = 2


def _gla_chunk(r0, c, refs, wa, q_scale):
    q_ref, k_ref, v_ref, g_ref, a_ref, bg_ref, on_ref, y_ref, s_ref = refs
    rows = pl.ds(r0, c)
    q = q_ref[rows, :] * q_scale
    k = k_ref[rows, :]
    v = v_ref[rows, :].astype(BF16)
    dk = q.shape[1]
    x = _dot(a_ref[rows, :].astype(BF16), wa) + bg_ref[...]
    la = _log_sigmoid(x) * (LOG2E / GLA_GATE_TAU)
    row = _iota2((c, c), 0)
    col = _iota2((c, c), 1)
    b = _sel_dot((col <= row).astype(BF16), la)
    b_last = b[c - 1:c, :]

    s_old = s_ref[...]
    o = _dot((q * jnp.exp2(b)).astype(BF16), s_old.astype(BF16))

    a_mat = jnp.where(row == col, jnp.sum(q * k, axis=1, keepdims=True), 0.0)
    xor = row ^ col
    sub = _iota2((c, dk), 0) % SUBLANES

    def tile_rows(first, stride, n):
        pieces = [jnp.broadcast_to(b[r:r + 1, :], (n, dk)) for r in range(first, c, stride)]
        return pieces[0] if len(pieces) == 1 else jnp.concatenate(pieces, axis=0)

    blk = 2
    while blk <= c:
        half = blk // 2
        if blk < SUBLANES:
            b_mid = tile_rows(half - 1, SUBLANES, SUBLANES)
            for s0 in range(blk, SUBLANES, blk):
                b_mid = jnp.where(sub >= s0, tile_rows(s0 + half - 1, SUBLANES, SUBLANES), b_mid)
        else:
            b_mid = tile_rows(half - 1, blk, blk)
        q_t = (q * jnp.exp2(b - b_mid)).astype(BF16)
        k_t = (k * jnp.exp2(b_mid - b)).astype(BF16)
        a_mat = jnp.where(xor >= half, _dot_nt(q_t, k_t), a_mat)
        blk *= 2
    a_mat = jnp.where(col <= row, a_mat, 0.0)
    o = o + _dot(a_mat.astype(BF16), v)

    k_end = (k * jnp.exp2(b_last - b)).astype(BF16)
    decay = jnp.transpose(jnp.broadcast_to(jnp.exp2(b_last), (8, dk)))[:, 0:1]
    s_ref[...] = decay * s_old + _dot_tn(k_end, v)

    o = o * lax.rsqrt(jnp.mean(o * o, axis=-1, keepdims=True) + EPS) * on_ref[...]
    y_ref[rows, :] = (o * _silu(g_ref[rows, :])).astype(y_ref.dtype)


def _gla_kernel(*refs, head, n_chunks, has_state, has_prev, q_scale):
    q_ref, k_ref, v_ref, g_ref, a_ref, wa_ref, bg_ref, on_ref = refs[:8]
    y_ref, so_ref, s_ref = refs[8 + has_state + has_prev:]
    if has_state:
        s_ref[...] = refs[8][...]
    else:
        s_ref[...] = jnp.zeros_like(s_ref)
    wa = wa_ref[...].astype(BF16)
    crefs = (q_ref, k_ref, v_ref, g_ref, a_ref, bg_ref, on_ref, y_ref, s_ref)
    if head:
        _gla_chunk(0, head, crefs, wa, q_scale)

    def body(i, carry):
        _gla_chunk(pl.multiple_of(head + i * GLA_CHUNK, 16), GLA_CHUNK, crefs, wa, q_scale)
        return carry

    if n_chunks:
        lax.fori_loop(0, n_chunks, body, 0, unroll=GLA_UNROLL if n_chunks % GLA_UNROLL == 0 else 1)
    so_ref[...] = s_ref[...]


def _gla(zq, zk, zv, zg, za, wa2, b_gate, onorm, state0, layer, n_layers, s_prev, n_heads):
    b, t, kw = zq.shape
    vw = zv.shape[2]
    dk, dv = kw // n_heads, vw // n_heads
    head = t % GLA_CHUNK
    n_chunks = t // GLA_CHUNK
    assert head in (0, 8, 16, 32, 64)
    tok = lambda last: pl.BlockSpec((None, t, last), lambda bi, hi: (bi, 0, hi))
    in_specs = [tok(dk), tok(dk), tok(dv), tok(dv),
                pl.BlockSpec((None, t, LANES), lambda bi, hi: (bi, 0, 0)),
                pl.BlockSpec((LANES, dk), lambda bi, hi: (0, hi)),
                pl.BlockSpec((1, dk), lambda bi, hi: (0, hi)),
                pl.BlockSpec((1, dv), lambda bi, hi: (0, 0))]
    args = [zq, zk, zv, zg, za, wa2, b_gate, onorm]
    if state0 is not None:
        in_specs.append(pl.BlockSpec((None, None, None, dk, dv), lambda bi, hi: (layer, bi, hi, 0, 0)))
        args.append(state0)
    aliases = {}
    if s_prev is not None:
        in_specs.append(pl.BlockSpec(memory_space=pl.ANY))
        aliases = {len(args): 1}
        args.append(s_prev)
    return pl.pallas_call(
        functools.partial(_gla_kernel, head=head, n_chunks=n_chunks, has_state=state0 is not None,
                          has_prev=s_prev is not None, q_scale=dk ** -0.5),
        grid=(b, n_heads),
        in_specs=in_specs,
        out_specs=[tok(dv),
                   pl.BlockSpec((None, None, None, dk, dv), lambda bi, hi: (layer, bi, hi, 0, 0))],
        out_shape=[jax.ShapeDtypeStruct((b, t, vw), BF16),
                   jax.ShapeDtypeStruct((n_layers, b, n_heads, dk, dv), F32)],
        scratch_shapes=[pltpu.VMEM((dk, dv), F32)],
        input_output_aliases=aliases,
        compiler_params=_cparams("arbitrary", "arbitrary"),
        name="gla",
    )(*args)


def _pad_rows(x):
    return jnp.pad(x, ((0, LANES - x.shape[0]), (0, 0)))


def _fox_layer(h_p, h_s, dims, norm, w_in_t, b_f, w_out, layer, n_heads, prev, sample_ctx):
    bsz, t, db, ds = dims
    d = h_p.shape[1]
    aw = w_out.shape[1]
    n_layers = w_out.shape[0]
    dh = aw // n_heads
    kp_prev, vp_prev, ks_prev, vs_prev = prev
    w_tail = _pad_rows(w_in_t[layer, 4 * aw:, :])
    u_p, zf_p = _rmsnorm_tail(h_p, norm[layer], w_tail)
    u_s, zf_s = _rmsnorm_tail(h_s, norm[layer], w_tail)
    proj_in = functools.partial(_proj, u_p, u_s, w_in_t, layer, w_is_nk=True)
    q_p, q_s = proj_in(0, aw, BF16, out_scale=dh ** -0.5 * LOG2E)
    k_p, k_s = proj_in(aw, aw, F32, stack=(n_layers, kp_prev, ks_prev))
    v_p, v_s = proj_in(2 * aw, aw, F32, stack=(n_layers, vp_prev, vs_prev))
    g_p, g_s = proj_in(3 * aw, aw, F32)
    bias = jnp.pad(b_f[layer][None, :].astype(F32), ((0, 0), (0, LANES - n_heads)))
    lf_p, c_p = _logf_cumsum(zf_p, bias, bsz, t)
    lf_s, c_s = _logf_cumsum(zf_s, bias, db, ds)
    lf_p = lf_p[:, :n_heads].reshape(bsz, t, n_heads)
    lf_s = lf_s[:, :n_heads].reshape(db, ds, n_heads)
    c_t = jnp.transpose(c_p[:, :n_heads].reshape(bsz, t, n_heads), (0, 2, 1))[:, :, None, :]
    y_p = _fox_prompt(q_p.reshape(bsz, t, aw), k_p.reshape(n_layers, bsz, t, aw),
                      v_p.reshape(n_layers, bsz, t, aw), g_p.reshape(bsz, t, aw),
                      c_t[..., :N_META], c_t[..., N_META:], layer, n_heads)
    cache_k, cache_v, cache_lf_t, page_table = sample_ctx
    y_s = _fox_sample(q_s.reshape(db, ds, aw), g_s.reshape(db, ds, aw), k_s, v_s,
                      c_s[:, :n_heads].reshape(db, ds, n_heads),
                      cache_k, cache_v, cache_lf_t, layer, page_table, n_heads)
    h_p, h_s = _proj(y_p.reshape(bsz * t, aw), y_s.reshape(db * ds, aw), w_out, layer, 0, d, F32,
                     w_is_nk=False, residual=(h_p, h_s))
    return h_p, h_s, (k_p, v_p, k_s, v_s), lf_p, lf_s


def _gla_layer(h_p, h_s, dims, norm, w_in_t, w_a2, b_gate, onorm, w_out, layer, n_heads, state0, prev):
    bsz, t, db, ds = dims
    d = h_p.shape[1]
    vw = w_out.shape[1]
    kw = w_a2.shape[2]
    n_layers = w_out.shape[0]
    w_tail = _pad_rows(w_in_t[layer, 2 * kw + 2 * vw:, :])
    u_p, za_p = _rmsnorm_tail(h_p, norm[layer], w_tail)
    u_s, za_s = _rmsnorm_tail(h_s, norm[layer], w_tail)
    proj_in = functools.partial(_proj, u_p, u_s, w_in_t, layer, w_is_nk=True)
    zq_p, zq_s = proj_in(0, kw, F32)
    zk_p, zk_s = proj_in(kw, kw, F32)
    zv_p, zv_s = proj_in(2 * kw, vw, F32)
    zg_p, zg_s = proj_in(2 * kw + vw, vw, F32)
    wa2 = _pad_rows(w_a2[layer])
    bg, on = b_gate[layer][None, :], onorm[layer][None, :]
    y_p, st_p = _gla(zq_p.reshape(bsz, t, kw), zk_p.reshape(bsz, t, kw), zv_p.reshape(bsz, t, vw),
                     zg_p.reshape(bsz, t, vw), za_p.reshape(bsz, t, LANES), wa2, bg, on,
                     None, layer, n_layers, prev[0], n_heads)
    y_s, st_s = _gla(zq_s.reshape(db, ds, kw), zk_s.reshape(db, ds, kw), zv_s.reshape(db, ds, vw),
                     zg_s.reshape(db, ds, vw), za_s.reshape(db, ds, LANES), wa2, bg, on,
                     state0, layer, n_layers, prev[1], n_heads)
    h_p, h_s = _proj(y_p.reshape(bsz * t, vw), y_s.reshape(db * ds, vw), w_out, layer, 0, d, F32,
                     w_is_nk=False, residual=(h_p, h_s))
    return h_p, h_s, (st_p, st_s)


def kernel(x_prompt, x_sample, cache_k, cache_v, cache_logf, state_gla, page_table, meta_tokens,
           norm_a, w_in_a, b_forget, w_out_a, norm_b, w_in_b, w_gate_up, b_gate, onorm_b, w_out_b,
           final_norm):
    bsz, seq, d = x_prompt.shape
    db, ds, _ = x_sample.shape
    a_heads = b_forget.shape[1]
    b_heads = state_gla.shape[2]
    dh = w_out_a.shape[1] // a_heads
    depth = norm_a.shape[0] + norm_b.shape[0]
    t = N_META + seq
    meta = jnp.broadcast_to(meta_tokens.astype(x_prompt.dtype)[None], (bsz, N_META, d))
    h_p = jnp.concatenate([meta, x_prompt], axis=1).reshape(bsz * t, d)
    h_s = x_sample.reshape(db * ds, d)
    w_in_a_t = jnp.swapaxes(w_in_a, 1, 2)
    w_in_b_t = jnp.swapaxes(w_in_b, 1, 2)
    cache_lf_t = jnp.swapaxes(cache_logf, 2, 3)
    dims = (bsz, t, db, ds)
    kv = (None, None, None, None)
    states = (None, None)
    nf_p, nf_s = [], []
    for i in range(depth):
        j = i // 2
        if i % 2 == 0:
            h_p, h_s, kv, lf_p, lf_s = _fox_layer(
                h_p, h_s, dims, norm_a, w_in_a_t, b_forget, w_out_a, j, a_heads, kv,
                (cache_k, cache_v, cache_lf_t, page_table))
            nf_p.append(lf_p)
            nf_s.append(lf_s)
        else:
            h_p, h_s, states = _gla_layer(
                h_p, h_s, dims, norm_b, w_in_b_t, w_gate_up, b_gate, onorm_b, w_out_b, j, b_heads,
                state_gla, states)
    y_prompt = _rmsnorm(h_p, final_norm, F32).reshape(bsz, t, d)[:, N_META:]
    y_sample = _rmsnorm(h_s, final_norm, F32).reshape(db, ds, d)
    n_a = norm_a.shape[0]
    kp_shape = (n_a, bsz, t, a_heads, dh)
    ks_shape = (n_a, db, ds, a_heads, dh)
    return (y_prompt, y_sample, kv[0].reshape(kp_shape), kv[1].reshape(kp_shape), jnp.stack(nf_p),
            kv[2].reshape(ks_shape), kv[3].reshape(ks_shape), jnp.stack(nf_s), states[0], states[1])
```
